```python
import jax, jax.numpy as jnp
from jax import lax
import numpy as np


D_MODEL = 2048
BATCH = 16
SEQ = 256
DEPTH = 2
DEC_BATCH = 4
DEC_SEQ = 4096
PAST_LEN = 512

GRID_W = 64
D_CONV_A = 1024
CONV_A_WIDTH = 3
N_HEADS = 16
QK_NOPE_DIM = 128
QK_ROPE_DIM = 64
V_HEAD_DIM = 128
KV_LORA_RANK = 512
QK_HEAD_DIM = QK_NOPE_DIM + QK_ROPE_DIM
N_FREQ = QK_ROPE_DIM // 4
ROPE_BASE = 10000.0
ATTN_SCALE = QK_HEAD_DIM ** -0.5
Q_BLOCK = 128
D_CONF = 1024
CONF_WIDTH = 31
N_BRANCHES = 3
D_FF = 4 * D_MODEL
NORM_EPS = 1e-6

SPLIT_A = 3 * D_CONV_A
SPLIT_Q = N_HEADS * QK_HEAD_DIM
SPLIT_KV = KV_LORA_RANK + QK_ROPE_DIM
SPLIT_CONF = 2 * D_CONF
SPLIT_GATE = N_BRANCHES * D_MODEL
OFF_Q = SPLIT_A
OFF_KV = OFF_Q + SPLIT_Q
OFF_CONF = OFF_KV + SPLIT_KV
OFF_GATE = OFF_CONF + SPLIT_CONF
D_IN_PROJ = OFF_GATE + SPLIT_GATE

kernel_name = 'hybrid_diffusion_parallel_gated_mla_conv_step'


def rms_norm(x, g):
    xf = x.astype(jnp.float32)
    y = xf * lax.rsqrt(jnp.mean(xf * xf, axis=-1, keepdims=True) + NORM_EPS)
    return (y * g.astype(jnp.float32)).astype(x.dtype)


def layer_norm(x, g, b):
    xf = x.astype(jnp.float32)
    mu = jnp.mean(xf, axis=-1, keepdims=True)
    xc = xf - mu
    var = jnp.mean(xc * xc, axis=-1, keepdims=True)
    y = xc * lax.rsqrt(var + NORM_EPS) * g.astype(jnp.float32) + b.astype(jnp.float32)
    return y.astype(x.dtype)


def depthwise_conv(x, w):
    k = w.shape[0]
    pad = (k - 1) // 2
    return lax.conv_general_dilated(
        x, w[:, None, :].astype(x.dtype), window_strides=(1,), padding=[(pad, pad)],
        dimension_numbers=('NWC', 'WIO', 'NWC'), feature_group_count=x.shape[-1])


def axial_rope_tables(n_tokens):
    rows = n_tokens // GRID_W
    row = jnp.repeat(jnp.arange(rows, dtype=jnp.float32), GRID_W)
    col = jnp.tile(jnp.arange(GRID_W, dtype=jnp.float32), rows)
    inv_freq = ROPE_BASE ** (-jnp.arange(N_FREQ, dtype=jnp.float32) / N_FREQ)
    ang = jnp.stack([row[:, None] * inv_freq, col[:, None] * inv_freq], axis=1)
    return jnp.cos(ang), jnp.sin(ang)


def apply_axial_rope(x, cos, sin):
    shp = x.shape
    xr = x.astype(jnp.float32).reshape(shp[:-1] + (2, 2, N_FREQ))
    x1, x2 = xr[..., 0, :], xr[..., 1, :]
    c, s = cos[:, None], sin[:, None]
    out = jnp.stack([x1 * c - x2 * s, x2 * c + x1 * s], axis=-2)
    return out.reshape(shp).astype(x.dtype)


def blocked_attention(q_nope, q_pe, k_nope, k_pe, v):
    b, sq, h, _ = q_nope.shape
    nblk = sq // Q_BLOCK

    def to_blocks(t):
        return jnp.moveaxis(t.reshape((b, nblk, Q_BLOCK) + t.shape[2:]), 1, 0)

    def one_block(qs):
        qn, qp = qs
        s = (jnp.einsum('bqhd,bkhd->bhqk', qn, k_nope, preferred_element_type=jnp.float32)
             + jnp.einsum('bqhr,bkr->bhqk', qp, k_pe, preferred_element_type=jnp.float32)) * ATTN_SCALE
        p = jax.nn.softmax(s, axis=-1).astype(v.dtype)
        return jnp.einsum('bhqk,bkhd->bqhd', p, v)

    out = lax.map(one_block, (to_blocks(q_nope), to_blocks(q_pe)))
    return jnp.moveaxis(out, 0, 1).reshape(b, sq, h, V_HEAD_DIM)


def short_conv_mixer(a_all, conv_a_w, w_out_a):
    xin = a_all[..., :D_CONV_A]
    gate_b = a_all[..., D_CONV_A:2 * D_CONV_A]
    gate_c = a_all[..., 2 * D_CONV_A:]
    y = depthwise_conv(gate_c * xin, conv_a_w)
    return (gate_b * y) @ w_out_a


def mla_mixer(q_all, kv_all, rope, ctx_ckv, ctx_kpe, kv_norm_g, w_kv_b, w_o_attn):
    b, s, _ = q_all.shape
    q = q_all.reshape(b, s, N_HEADS, QK_HEAD_DIM)
    q_nope, q_pe = q[..., :QK_NOPE_DIM], q[..., QK_NOPE_DIM:]
    ckv = rms_norm(kv_all[..., :KV_LORA_RANK], kv_norm_g)
    kpe = kv_all[..., KV_LORA_RANK:]
    if rope is None:
        keys_ckv, keys_kpe = ckv, kpe
    else:
        cos, sin = rope
        q_pe = apply_axial_rope(q_pe, cos, sin)
        kpe_rot = apply_axial_rope(kpe[:, :, None, :], cos, sin)[:, :, 0, :]
        keys_ckv = jnp.concatenate([ckv, ctx_ckv.astype(ckv.dtype)], axis=1)
        keys_kpe = jnp.concatenate([kpe_rot, ctx_kpe.astype(kpe.dtype)], axis=1)
    kv = jnp.einsum('bkc,chd->bkhd', keys_ckv, w_kv_b)
    k_nope, v = kv[..., :QK_NOPE_DIM], kv[..., QK_NOPE_DIM:]
    o = blocked_attention(q_nope, q_pe, k_nope, keys_kpe, v)
    return jnp.einsum('bshd,hdm->bsm', o, w_o_attn), ckv, kpe


def conformer_conv_mixer(c_all, conf_dw_w, conf_dw_b, conf_ln_g, conf_ln_b, w_out_c):
    u = jax.nn.glu(c_all, axis=-1)
    u = depthwise_conv(u, conf_dw_w) + conf_dw_b
    u = jax.nn.silu(layer_norm(u, conf_ln_g, conf_ln_b))
    return u @ w_out_c


def trunk_layer(x, mod, rope, ctx_ckv, ctx_kpe, norm1_g, w_in, conv_a_w, w_out_a, kv_norm_g,
                w_kv_b, w_o_attn, conf_dw_w, conf_dw_b, conf_ln_g, conf_ln_b, w_out_c,
                w_merge, norm2_g, w_ff1, w_ff2):
    mod = mod.astype(x.dtype)
    shift1, scale1, gate1, shift2, scale2, gate2 = jnp.split(mod, 6, axis=-1)
    h = rms_norm(x, norm1_g) * (1 + scale1) + shift1
    proj = h @ w_in
    y_a = short_conv_mixer(proj[..., :OFF_Q], conv_a_w, w_out_a)
    y_b, ckv, kpe = mla_mixer(proj[..., OFF_Q:OFF_KV], proj[..., OFF_KV:OFF_CONF], rope,
                              ctx_ckv, ctx_kpe, kv_norm_g, w_kv_b, w_o_attn)
    y_c = conformer_conv_mixer(proj[..., OFF_CONF:OFF_GATE], conf_dw_w, conf_dw_b,
                               conf_ln_g, conf_ln_b, w_out_c)
    g = jax.nn.sigmoid(proj[..., OFF_GATE:])
    merged = (g[..., :D_MODEL] * y_a + g[..., D_MODEL:2 * D_MODEL] * y_b
              + g[..., 2 * D_MODEL:] * y_c)
    x = x + gate1 * (merged @ w_merge)
    h2 = rms_norm(x, norm2_g) * (1 + scale2) + shift2
    x = x + gate2 * (jnp.square(jax.nn.relu(h2 @ w_ff1)) @ w_ff2)
    return x, ckv, kpe


def setup_inputs(seed: int = 0) -> dict:
    key = jax.random.key(seed)
    ks = jax.random.split(key, 28)

    def nrm(k, shape, scale=1.0):
        return jax.random.normal(k, shape, jnp.float32) * scale

    return {
        'x_prompt': nrm(ks[0], (BATCH, SEQ, D_MODEL)),
        'x_sample': nrm(ks[1], (DEC_BATCH, DEC_SEQ, D_MODEL)),
        'cache_ckv': nrm(ks[2], (DEC_BATCH, DEPTH, PAST_LEN, KV_LORA_RANK)),
        'cache_kpe': nrm(ks[3], (DEC_BATCH, DEPTH, PAST_LEN, QK_ROPE_DIM)),
        'c': nrm(ks[4], (DEC_BATCH, D_MODEL)),
        'c_ctx': nrm(ks[5], (D_MODEL,)),
        'w_mod': nrm(ks[6], (DEPTH, D_MODEL, 6 * D_MODEL), 0.5 * D_MODEL ** -0.5),
        'b_mod': nrm(ks[7], (DEPTH, 6 * D_MODEL), 0.02),
        'norm1_g': 1.0 + nrm(ks[8], (DEPTH, D_MODEL), 0.02),
        'w_in': nrm(ks[9], (DEPTH, D_MODEL, D_IN_PROJ), D_MODEL ** -0.5),
        'conv_a_w': nrm(ks[10], (DEPTH, CONV_A_WIDTH, D_CONV_A), CONV_A_WIDTH ** -0.5),
        'w_out_a': nrm(ks[11], (DEPTH, D_CONV_A, D_MODEL), D_CONV_A ** -0.5),
        'kv_norm_g': 1.0 + nrm(ks[12], (DEPTH, KV_LORA_RANK), 0.02),
        'w_kv_b': nrm(ks[13], (DEPTH, KV_LORA_RANK, N_HEADS, QK_NOPE_DIM + V_HEAD_DIM), KV_LORA_RANK ** -0.5),
        'w_o_attn': nrm(ks[14], (DEPTH, N_HEADS, V_HEAD_DIM, D_MODEL), (N_HEADS * V_HEAD_DIM) ** -0.5),
        'conf_dw_w': nrm(ks[15], (DEPTH, CONF_WIDTH, D_CONF), CONF_WIDTH ** -0.5),
        'conf_dw_b': nrm(ks[16], (DEPTH, D_CONF), 0.02),
        'conf_ln_g': 1.0 + nrm(ks[17], (DEPTH, D_CONF), 0.02),
        'conf_ln_b': nrm(ks[18], (DEPTH, D_CONF), 0.02),
        'w_out_c': nrm(ks[19], (DEPTH, D_CONF, D_MODEL), D_CONF ** -0.5),
        'w_merge': nrm(ks[20], (DEPTH, D_MODEL, D_MODEL), D_MODEL ** -0.5),
        'norm2_g': 1.0 + nrm(ks[21], (DEPTH, D_MODEL), 0.02),
        'w_ff1': nrm(ks[22], (DEPTH, D_MODEL, D_FF), D_MODEL ** -0.5),
        'w_ff2': nrm(ks[23], (DEPTH, D_FF, D_MODEL), D_FF ** -0.5),
        'final_norm_g': 1.0 + nrm(ks[24], (D_MODEL,), 0.02),
    }


def reference(x_prompt, x_sample, cache_ckv, cache_kpe, c, c_ctx, w_mod, b_mod, norm1_g, w_in,
              conv_a_w, w_out_a, kv_norm_g, w_kv_b, w_o_attn, conf_dw_w, conf_dw_b, conf_ln_g,
              conf_ln_b, w_out_c, w_merge, norm2_g, w_ff1, w_ff2, final_norm_g):
    def run_layer(l, x, mod, rope, ctx_ckv, ctx_kpe):
        return trunk_layer(x, mod, rope, ctx_ckv, ctx_kpe,
                           norm1_g=norm1_g[l], w_in=w_in[l], conv_a_w=conv_a_w[l],
                           w_out_a=w_out_a[l], kv_norm_g=kv_norm_g[l], w_kv_b=w_kv_b[l],
                           w_o_attn=w_o_attn[l], conf_dw_w=conf_dw_w[l], conf_dw_b=conf_dw_b[l],
                           conf_ln_g=conf_ln_g[l], conf_ln_b=conf_ln_b[l], w_out_c=w_out_c[l],
                           w_merge=w_merge[l], norm2_g=norm2_g[l], w_ff1=w_ff1[l], w_ff2=w_ff2[l])

    xp = x_prompt
    ckv_layers, kpe_layers = [], []
    for l in range(DEPTH):
        mod = (jax.nn.silu(c_ctx) @ w_mod[l] + b_mod[l])[None, None, :]
        xp, ckv, kpe = run_layer(l, xp, mod, None, None, None)
        ckv_layers.append(ckv)
        kpe_layers.append(kpe)
    y_prompt = rms_norm(xp, final_norm_g)
    new_ckv = jnp.stack(ckv_layers, axis=1)
    new_kpe = jnp.stack(kpe_layers, axis=1)

    rope = axial_rope_tables(x_sample.shape[1])
    xs = x_sample
    for l in range(DEPTH):
        mod = (jax.nn.silu(c) @ w_mod[l] + b_mod[l])[:, None, :]
        xs, _, _ = run_layer(l, xs, mod, rope, cache_ckv[:, l], cache_kpe[:, l])
    y_sample = rms_norm(xs, final_norm_g)

    return (y_prompt, y_sample, new_ckv, new_kpe)
```

```python
import functools

import jax
import jax.numpy as jnp
from jax import lax
from jax.experimental import pallas as pl
from jax.experimental.pallas import tpu as pltpu

F32 = jnp.float32
BF16 = jnp.bfloat16

D_MODEL = 2048
DEPTH = 2
GRID_W = 64
D_CONV_A = 1024
CONV_A_WIDTH = 3
N_HEADS = 16
QK_NOPE_DIM = 128
QK_ROPE_DIM = 64
V_HEAD_DIM = 128
KV_LORA_RANK = 512
QK_HEAD_DIM = QK_NOPE_DIM + QK_ROPE_DIM
N_FREQ = QK_ROPE_DIM // 4
ROPE_BASE = 10000.0
ATTN_SCALE = QK_HEAD_DIM ** -0.5
D_CONF = 1024
CONF_WIDTH = 31
D_FF = 4 * D_MODEL
NORM_EPS = 1e-6

SPLIT_A = 3 * D_CONV_A
SPLIT_Q = N_HEADS * QK_HEAD_DIM
SPLIT_KV = KV_LORA_RANK + QK_ROPE_DIM
SPLIT_CONF = 2 * D_CONF
OFF_Q = SPLIT_A
OFF_KV = OFF_Q + SPLIT_Q
OFF_CONF = OFF_KV + SPLIT_KV
OFF_GATE = OFF_CONF + SPLIT_CONF

LANES = 128
BF16_ROWS = 16
MIB = 1024 * 1024

P_OFF_A = 0
P_OFF_QN = SPLIT_A
P_OFF_CONF = P_OFF_QN + N_HEADS * QK_NOPE_DIM
P_OFF_GATE = P_OFF_CONF + SPLIT_CONF
P_WIDTH = P_OFF_GATE + 3 * D_MODEL
S_OFF_CKV = N_HEADS * QK_ROPE_DIM
S_OFF_KPE = S_OFF_CKV + KV_LORA_RANK
S_WIDTH = S_OFF_KPE + LANES
KEY_WIDTH = 2 * LANES

ROW_CHUNK = 16
CONV_HALO = 16


def _cparams(n_axes, vmem_mib):
    return pltpu.CompilerParams(dimension_semantics=("arbitrary",) * n_axes,
                                vmem_limit_bytes=vmem_mib * MIB)


def _mod_kernel(c_ref, w_ref, b_ref, o_ref):
    c = c_ref[...]
    s = c * jax.nn.sigmoid(c)
    o_ref[...] = jnp.dot(s.astype(BF16), w_ref[...].astype(BF16),
                         preferred_element_type=F32) + b_ref[...]


def _mod_call(c_rows, w_mod, b_mod):
    n_rows = c_rows.shape[0]
    n = 6 * D_MODEL
    tn = 1024
    return pl.pallas_call(
        _mod_kernel,
        grid=(DEPTH, n // tn),
        in_specs=[pl.BlockSpec((n_rows, D_MODEL), lambda l, j: (0, 0)),
                  pl.BlockSpec((None, D_MODEL, tn), lambda l, j: (l, 0, j)),
                  pl.BlockSpec((None, 1, tn), lambda l, j: (l, 0, j))],
        out_specs=pl.BlockSpec((None, n_rows, tn), lambda l, j: (l, 0, j)),
        out_shape=jax.ShapeDtypeStruct((DEPTH, n_rows, n), F32),
        compiler_params=_cparams(2, 40),
        name="mod",
    )(c_rows, w_mod, b_mod.reshape(DEPTH, 1, n))


def _modulated_rmsnorm_rows(x, g, scale, shift):
    ms = jnp.mean(x * x, axis=-1, keepdims=True)
    y = x * lax.rsqrt(ms + NORM_EPS) * g
    return y * (1.0 + scale) + shift


def _rope_lanes(x, cos, s_up, s_dn):
    up = pltpu.roll(x, LANES - N_FREQ, 1)
    dn = pltpu.roll(x, N_FREQ, 1)
    return x * cos + up * s_up + dn * s_dn


def _inproj_kernel(x_ref, mod_ref, g_ref, w_ref, p_ref, h_ref):
    @pl.when(pl.program_id(1) == 0)
    def _():
        g = g_ref[...]
        shift = mod_ref[0:1, :]
        scale = mod_ref[1:2, :]

        def body(r, carry):
            rows = pl.ds(pl.multiple_of(r * ROW_CHUNK, ROW_CHUNK), ROW_CHUNK)
            h = _modulated_rmsnorm_rows(x_ref[rows, :], g, scale, shift)
            h_ref[rows, :] = h.astype(BF16)
            return carry

        lax.fori_loop(0, x_ref.shape[0] // ROW_CHUNK, body, 0)

    p_ref[...] = jnp.dot(h_ref[...], w_ref[...], preferred_element_type=F32).astype(BF16)


def _inproj_call(x, mod_l, g1, w_main, row_fn, tm):
    t = x.shape[0]
    tn = 1024
    return pl.pallas_call(
        _inproj_kernel,
        grid=(t // tm, P_WIDTH // tn),
        in_specs=[pl.BlockSpec((tm, D_MODEL), lambda i, j: (i, 0)),
                  pl.BlockSpec((None, 6, D_MODEL), lambda i, j: (row_fn(i * tm), 0, 0)),
                  pl.BlockSpec((1, D_MODEL), lambda i, j: (0, 0)),
                  pl.BlockSpec((D_MODEL, tn), lambda i, j: (0, j))],
        out_specs=[pl.BlockSpec((tm, tn), lambda i, j: (i, j)),
                   pl.BlockSpec((tm, D_MODEL), lambda i, j: (i, 0))],
        out_shape=[jax.ShapeDtypeStruct((t, P_WIDTH), BF16),
                   jax.ShapeDtypeStruct((t, D_MODEL), BF16)],
        compiler_params=_cparams(2, 48),
        name="inproj",
    )(x, mod_l, g1, w_main)


def _qkvs_kernel(*refs, rope):
    if rope:
        h_ref, w_ref, kvg_ref, cos_ref, sup_ref, sdn_ref, qpe_ref, ckv_ref, kpe_ref, kpr_ref = refs
    else:
        h_ref, w_ref, kvg_ref, qpe_ref, ckv_ref, kpe_ref = refs
    acc = jnp.dot(h_ref[...], w_ref[...], preferred_element_type=F32)
    if rope:
        cos, s_up, s_dn = cos_ref[...], sup_ref[...], sdn_ref[...]
    for c in range(S_OFF_CKV // LANES):
        q = acc[:, c * LANES:(c + 1) * LANES]
        if rope:
            q = _rope_lanes(q, cos, s_up, s_dn)
        qpe_ref[:, c * LANES:(c + 1) * LANES] = q.astype(BF16)
    ckv = acc[:, S_OFF_CKV:S_OFF_KPE]
    ms = jnp.mean(ckv * ckv, axis=-1, keepdims=True)
    ckv_ref[...] = ckv * lax.rsqrt(ms + NORM_EPS) * kvg_ref[...]
    kpe2 = acc[:, S_OFF_KPE:S_WIDTH]
    kpe_ref[...] = kpe2[:, :QK_ROPE_DIM]
    if rope:
        kpr_ref[...] = _rope_lanes(kpe2, cos, s_up, s_dn)[:, :QK_ROPE_DIM]


def _qkvs_call(h, w_small, kv_g, rope_tabs, tm):
    t = h.shape[0]
    rope = rope_tabs is not None
    in_specs = [pl.BlockSpec((tm, D_MODEL), lambda i: (i, 0)),
                pl.BlockSpec((D_MODEL, S_WIDTH), lambda i: (0, 0)),
                pl.BlockSpec((1, KV_LORA_RANK), lambda i: (0, 0))]
    args = [h, w_small, kv_g]
    out_specs = [pl.BlockSpec((tm, S_OFF_CKV), lambda i: (i, 0)),
                 pl.BlockSpec((tm, KV_LORA_RANK), lambda i: (i, 0)),
                 pl.BlockSpec((tm, QK_ROPE_DIM), lambda i: (i, 0))]
    out_shape = [jax.ShapeDtypeStruct((t, S_OFF_CKV), BF16),
                 jax.ShapeDtypeStruct((t, KV_LORA_RANK), F32),
                 jax.ShapeDtypeStruct((t, QK_ROPE_DIM), F32)]
    if rope:
        seq_tiles = rope_tabs[0].shape[0] // tm
        in_specs += [pl.BlockSpec((tm, LANES), lambda i: (i % seq_tiles, 0))] * 3
        args += list(rope_tabs)
        out_specs.append(pl.BlockSpec((tm, QK_ROPE_DIM), lambda i: (i, 0)))
        out_shape.append(jax.ShapeDtypeStruct((t, QK_ROPE_DIM), F32))
    return pl.pallas_call(
        functools.partial(_qkvs_kernel, rope=rope),
        grid=(t // tm,),
        in_specs=in_specs, out_specs=out_specs, out_shape=out_shape,
        compiler_params=_cparams(1, 40),
        name="qkvs",
    )(*args)


def _mixers_kernel(xin, gb, gc, ca, cb, xin_p, gc_p, ca_p, cb_p, xin_n, gc_n, ca_n, cb_n,
                   wa_ref, wc_ref, cbias_ref, lng_ref, lnb_ref, za_ref, uc_ref,
                   bufa, bufc, convc, *, ts, tiles_per_seq):
    i = pl.program_id(0)
    first = (i % tiles_per_seq) == 0
    last = (i % tiles_per_seq) == tiles_per_seq - 1
    hl = CONV_HALO

    def f32(r):
        return r[...].astype(F32)

    bufa[0:hl, :] = jnp.where(first, 0.0, f32(gc_p) * f32(xin_p))
    bufa[hl:hl + ts, :] = f32(gc) * f32(xin)
    bufa[hl + ts:, :] = jnp.where(last, 0.0, f32(gc_n) * f32(xin_n))
    pad_a = (CONV_A_WIDTH - 1) // 2
    y = jnp.zeros((ts, D_CONV_A), F32)
    for k in range(CONV_A_WIDTH):
        y = y + wa_ref[k:k + 1, :] * bufa[pl.ds(hl - pad_a + k, ts), :]
    za_ref[...] = (f32(gb) * y).astype(BF16)

    def glu(a, b):
        return f32(a) * jax.nn.sigmoid(f32(b))

    n_chunks = D_CONF // LANES
    up = jnp.where(first, 0.0, glu(ca_p, cb_p))
    um = glu(ca, cb)
    un = jnp.where(last, 0.0, glu(ca_n, cb_n))
    for c in range(n_chunks):
        lanes = slice(c * LANES, (c + 1) * LANES)
        bufc[c, 0:hl, :] = up[:, lanes]
        bufc[c, hl:hl + ts, :] = um[:, lanes]
        bufc[c, hl + ts:, :] = un[:, lanes]
    pad_c = (CONF_WIDTH - 1) // 2

    def chunk_body(c, carry):
        w = wc_ref[c]
        acc = jnp.zeros((ts, LANES), F32)
        for k in range(CONF_WIDTH):
            acc = acc + w[k:k + 1, :] * bufc[c, pl.ds(hl - pad_c + k, ts), :]
        convc[c] = acc
        return carry

    lax.fori_loop(0, n_chunks, chunk_body, 0)
    u = jnp.concatenate([convc[c] for c in range(n_chunks)], axis=1) + cbias_ref[...]
    mu = jnp.mean(u, axis=-1, keepdims=True)
    uc = u - mu
    var = jnp.mean(uc * uc, axis=-1, keepdims=True)
    v = uc * lax.rsqrt(var + NORM_EPS) * lng_ref[...] + lnb_ref[...]
    uc_ref[...] = (v * jax.nn.sigmoid(v)).astype(BF16)


def _mixers_call(p, conv_a_w, conf_w_chunks, conf_b, ln_g, ln_b, seq_len, ts):
    t = p.shape[0]
    tiles_per_seq = seq_len // ts
    halo_per_tile = ts // CONV_HALO
    n_halo_blocks = t // CONV_HALO
    col_a = P_OFF_A // D_CONV_A
    col_c = P_OFF_CONF // D_CONF
    cols = [col_a, col_a + 1, col_a + 2, col_c, col_c + 1]
    halo_cols = [col_a, col_a + 2, col_c, col_c + 1]

    def main_spec(col):
        return pl.BlockSpec((ts, D_CONF), lambda i: (i, col))

    def prev_spec(col):
        return pl.BlockSpec((CONV_HALO, D_CONF),
                            lambda i: (jnp.maximum(i * halo_per_tile - 1, 0), col))

    def next_spec(col):
        return pl.BlockSpec((CONV_HALO, D_CONF),
                            lambda i: (jnp.minimum((i + 1) * halo_per_tile, n_halo_blocks - 1), col))

    def full(a):
        return pl.BlockSpec(a.shape, lambda i: (0,) * a.ndim)

    small = [conv_a_w, conf_w_chunks, conf_b, ln_g, ln_b]
    in_specs = ([main_spec(c) for c in cols] + [prev_spec(c) for c in halo_cols]
                + [next_spec(c) for c in halo_cols] + [full(a) for a in small])
    return pl.pallas_call(
        functools.partial(_mixers_kernel, ts=ts, tiles_per_seq=tiles_per_seq),
        grid=(t // ts,),
        in_specs=in_specs,
        out_specs=[pl.BlockSpec((ts, D_CONV_A), lambda i: (i, 0)),
                   pl.BlockSpec((ts, D_CONF), lambda i: (i, 0))],
        out_shape=[jax.ShapeDtypeStruct((t, D_CONV_A), BF16),
                   jax.ShapeDtypeStruct((t, D_CONF), BF16)],
        scratch_shapes=[pltpu.VMEM((ts + 2 * CONV_HALO, D_CONV_A), F32),
                        pltpu.VMEM((D_CONF // LANES, ts + 2 * CONV_HALO, LANES), F32),
                        pltpu.VMEM((D_CONF // LANES, ts, LANES), F32)],
        compiler_params=_cparams(1, 40),
        name="mixers",
    )(*([p] * 13), *small)


def _kvup_kernel(ckv_ref, kpe_ref, w_ref, k_ref, v_ref):
    ckv = ckv_ref[...].astype(BF16)
    kpe = kpe_ref[...].astype(BF16)
    zeros = jnp.zeros_like(kpe)
    for h in range(N_HEADS):
        kv = jnp.dot(ckv, w_ref[h], preferred_element_type=F32)
        k_ref[h, :, 0:QK_NOPE_DIM] = kv[:, :QK_NOPE_DIM].astype(BF16)
        lo, hi = (kpe, zeros) if h % 2 == 0 else (zeros, kpe)
        k_ref[h, :, QK_NOPE_DIM:QK_NOPE_DIM + QK_ROPE_DIM] = lo
        k_ref[h, :, QK_NOPE_DIM + QK_ROPE_DIM:] = hi
        v_ref[h] = kv[:, QK_NOPE_DIM:].astype(BF16)


def _kvup_call(ckv_all, kpe_all, w_kvb, tk):
    b, sk, _ = ckv_all.shape
    return pl.pallas_call(
        _kvup_kernel,
        grid=(b, sk // tk),
        in_specs=[pl.BlockSpec((None, tk, KV_LORA_RANK), lambda bi, s: (bi, s, 0)),
                  pl.BlockSpec((None, tk, QK_ROPE_DIM), lambda bi, s: (bi, s, 0)),
                  pl.BlockSpec(w_kvb.shape, lambda bi, s: (0, 0, 0))],
        out_specs=[pl.BlockSpec((None, N_HEADS, tk, KEY_WIDTH), lambda bi, s: (bi, 0, s, 0)),
                   pl.BlockSpec((None, N_HEADS, tk, V_HEAD_DIM), lambda bi, s: (bi, 0, s, 0))],
        out_shape=[jax.ShapeDtypeStruct((b, N_HEADS, sk, KEY_WIDTH), BF16),
                   jax.ShapeDtypeStruct((b, N_HEADS, sk, V_HEAD_DIM), BF16)],
        compiler_params=_cparams(2, 40),
        name="kvup",
    )(ckv_all, kpe_all, w_kvb)


def _attn_kernel(qn_ref, qp_ref, k_ref, v_ref, o_ref):
    qp = qp_ref[...]
    for h in range(2):
        q = jnp.concatenate([qn_ref[:, h * QK_NOPE_DIM:(h + 1) * QK_NOPE_DIM], qp], axis=1)
        s = lax.dot_general(q, k_ref[h], (((1,), (1,)), ((), ())), preferred_element_type=F32)
        m = jnp.max(s, axis=-1, keepdims=True)
        p = jnp.exp((s - m) * ATTN_SCALE)
        l = jnp.sum(p, axis=-1, keepdims=True)
        o = jnp.dot(p.astype(BF16), v_ref[h], preferred_element_type=F32)
        o_ref[:, h * V_HEAD_DIM:(h + 1) * V_HEAD_DIM] = (o * (1.0 / l)).astype(BF16)


def _attn_call(p, qpe, k, v, seq_len, tq):
    t = p.shape[0]
    sk = k.shape[2]
    q_tiles = seq_len // tq
    qn_col0 = P_OFF_QN // (2 * QK_NOPE_DIM)
    return pl.pallas_call(
        _attn_kernel,
        grid=(t // seq_len, N_HEADS // 2, q_tiles),
        in_specs=[pl.BlockSpec((tq, 2 * QK_NOPE_DIM), lambda b, hp, qi: (b * q_tiles + qi, qn_col0 + hp)),
                  pl.BlockSpec((tq, 2 * QK_ROPE_DIM), lambda b, hp, qi: (b * q_tiles + qi, hp)),
                  pl.BlockSpec((None, 2, sk, KEY_WIDTH), lambda b, hp, qi: (b, hp, 0, 0)),
                  pl.BlockSpec((None, 2, sk, V_HEAD_DIM), lambda b, hp, qi: (b, hp, 0, 0))],
        out_specs=pl.BlockSpec((tq, 2 * V_HEAD_DIM), lambda b, hp, qi: (b * q_tiles + qi, hp)),
        out_shape=jax.ShapeDtypeStruct((t, N_HEADS * V_HEAD_DIM), BF16),
        compiler_params=_cparams(3, 48),
        name="attn",
    )(p, qpe, k, v)


def _merge_kernel(za_ref, o_ref, uc_ref, g0_ref, g1_ref, g2_ref, x_ref, mod_ref,
                  woa_ref, wo_ref, woc_ref, wm_ref, out_ref):
    j = pl.program_id(1)

    def sig(r):
        return jax.nn.sigmoid(r[...].astype(F32))

    ya = jnp.dot(za_ref[...], woa_ref[...], preferred_element_type=F32)
    yb = jnp.dot(o_ref[...], wo_ref[...], preferred_element_type=F32)
    yc = jnp.dot(uc_ref[...], woc_ref[...], preferred_element_type=F32)
    merged = sig(g0_ref) * ya + sig(g1_ref) * yb + sig(g2_ref) * yc
    contrib = jnp.dot(merged.astype(BF16), wm_ref[...], preferred_element_type=F32)

    @pl.when(j == 0)
    def _():
        out_ref[...] = contrib

    @pl.when(j > 0)
    def _():
        out_ref[...] += contrib

    @pl.when(j == pl.num_programs(1) - 1)
    def _():
        gate1 = mod_ref[2:3, :]

        def body(r, carry):
            rows = pl.ds(pl.multiple_of(r * ROW_CHUNK, ROW_CHUNK), ROW_CHUNK)
            out_ref[rows, :] = x_ref[rows, :] + gate1 * out_ref[rows, :]
            return carry

        lax.fori_loop(0, out_ref.shape[0] // ROW_CHUNK, body, 0)


def _merge_call(za, o, uc, p, x, mod_l, w_out_a, w_o, w_out_c, w_merge, row_fn, tm):
    t = x.shape[0]
    tn = 512
    gate_col0 = P_OFF_GATE // tn
    per_gate = D_MODEL // tn

    def gate_spec(g):
        return pl.BlockSpec((tm, tn), lambda i, j: (i, gate_col0 + g * per_gate + j))

    return pl.pallas_call(
        _merge_kernel,
        grid=(t // tm, D_MODEL // tn),
        in_specs=[pl.BlockSpec((tm, D_CONV_A), lambda i, j: (i, 0)),
                  pl.BlockSpec((tm, D_MODEL), lambda i, j: (i, 0)),
                  pl.BlockSpec((tm, D_CONF), lambda i, j: (i, 0)),
                  gate_spec(0), gate_spec(1), gate_spec(2),
                  pl.BlockSpec((tm, D_MODEL), lambda i, j: (i, 0)),
                  pl.BlockSpec((None, 6, D_MODEL), lambda i, j: (row_fn(i * tm), 0, 0)),
                  pl.BlockSpec((D_CONV_A, tn), lambda i, j: (0, j)),
                  pl.BlockSpec((D_MODEL, tn), lambda i, j: (0, j)),
                  pl.BlockSpec((D_CONF, tn), lambda i, j: (0, j)),
                  pl.BlockSpec((tn, D_MODEL), lambda i, j: (j, 0))],
        out_specs=pl.BlockSpec((tm, D_MODEL), lambda i, j: (i, 0)),
        out_shape=jax.ShapeDtypeStruct((t, D_MODEL), F32),
        compiler_params=_cparams(2, 52),
        name="merge",
    )(za, o, uc, p, p, p, x, mod_l, w_out_a, w_o, w_out_c, w_merge)


def _ffn_kernel(*refs, final_norm):
    if final_norm:
        x_ref, mod_ref, g_ref, w1_ref, w2_ref, fg_ref, out_ref, h_ref = refs
    else:
        x_ref, mod_ref, g_ref, w1_ref, w2_ref, out_ref, h_ref = refs
    k = pl.program_id(1)
    n_chunks = x_ref.shape[0] // ROW_CHUNK

    @pl.when(k == 0)
    def _():
        g = g_ref[...]
        shift = mod_ref[3:4, :]
        scale = mod_ref[4:5, :]

        def body(r, carry):
            rows = pl.ds(pl.multiple_of(r * ROW_CHUNK, ROW_CHUNK), ROW_CHUNK)
            h_ref[rows, :] = _modulated_rmsnorm_rows(x_ref[rows, :], g, scale, shift).astype(BF16)
            return carry

        lax.fori_loop(0, n_chunks, body, 0)

    a = jnp.dot(h_ref[...], w1_ref[...], preferred_element_type=F32)
    a = jnp.square(jnp.maximum(a, 0.0)).astype(BF16)
    contrib = jnp.dot(a, w2_ref[...], preferred_element_type=F32)

    @pl.when(k == 0)
    def _():
        out_ref[...] = contrib

    @pl.when(k > 0)
    def _():
        out_ref[...] += contrib

    @pl.when(k == pl.num_programs(1) - 1)
    def _():
        gate2 = mod_ref[5:6, :]

        def body(r, carry):
            rows = pl.ds(pl.multiple_of(r * ROW_CHUNK, ROW_CHUNK), ROW_CHUNK)
            y = x_ref[rows, :] + gate2 * out_ref[rows, :]
            if final_norm:
                ms = jnp.mean(y * y, axis=-1, keepdims=True)
                y = y * lax.rsqrt(ms + NORM_EPS) * fg_ref[...]
            out_ref[rows, :] = y
            return carry

        lax.fori_loop(0, n_chunks, body, 0)


def _ffn_call(x, mod_l, g2, w1, w2, final_g, row_fn, tm):
    t = x.shape[0]
    tf = 512
    final_norm = final_g is not None
    in_specs = [pl.BlockSpec((tm, D_MODEL), lambda i, k: (i, 0)),
                pl.BlockSpec((None, 6, D_MODEL), lambda i, k: (row_fn(i * tm), 0, 0)),
                pl.BlockSpec((1, D_MODEL), lambda i, k: (0, 0)),
                pl.BlockSpec((D_MODEL, tf), lambda i, k: (0, k)),
                pl.BlockSpec((tf, D_MODEL), lambda i, k: (k, 0))]
    args = [x, mod_l, g2, w1, w2]
    if final_norm:
        in_specs.append(pl.BlockSpec((1, D_MODEL), lambda i, k: (0, 0)))
        args.append(final_g)
    return pl.pallas_call(
        functools.partial(_ffn_kernel, final_norm=final_norm),
        grid=(t // tm, D_FF // tf),
        in_specs=in_specs,
        out_specs=pl.BlockSpec((tm, D_MODEL), lambda i, k: (i, 0)),
        out_shape=jax.ShapeDtypeStruct((t, D_MODEL), F32),
        scratch_shapes=[pltpu.VMEM((tm, D_MODEL), BF16)],
        compiler_params=_cparams(2, 52),
        name="ffn",
    )(*args)


def _rope_tables(n_tokens):
    pos = jnp.arange(n_tokens)
    row = (pos // GRID_W).astype(F32)
    col = (pos % GRID_W).astype(F32)
    inv_freq = ROPE_BASE ** (-jnp.arange(N_FREQ, dtype=F32) / N_FREQ)
    ang = jnp.stack([row[:, None] * inv_freq, col[:, None] * inv_freq], axis=1)
    cos, sin = jnp.cos(ang), jnp.sin(ang)
    zero = jnp.zeros_like(sin)
    cos64 = jnp.stack([cos, cos], axis=2).reshape(n_tokens, QK_ROPE_DIM)
    up64 = jnp.stack([-sin, zero], axis=2).reshape(n_tokens, QK_ROPE_DIM)
    dn64 = jnp.stack([zero, sin], axis=2).reshape(n_tokens, QK_ROPE_DIM)
    return tuple(jnp.concatenate([a, a], axis=1) for a in (cos64, up64, dn64))


def _prep_layer_weights(l, w_in, w_kv_b, w_o_attn, conf_dw_w):
    w = w_in[l]
    w_q = w[:, OFF_Q:OFF_KV].reshape(D_MODEL, N_HEADS, QK_HEAD_DIM)
    w_qn = w_q[:, :, :QK_NOPE_DIM].reshape(D_MODEL, N_HEADS * QK_NOPE_DIM)
    w_qp = w_q[:, :, QK_NOPE_DIM:].reshape(D_MODEL, N_HEADS * QK_ROPE_DIM)
    w_main = jnp.concatenate([w[:, :OFF_Q], w_qn, w[:, OFF_CONF:]], axis=1).astype(BF16)
    w_small = jnp.concatenate(
        [w_qp, w[:, OFF_KV:OFF_CONF], jnp.zeros((D_MODEL, LANES - QK_ROPE_DIM), F32)], axis=1).astype(BF16)
    w_kvb = jnp.transpose(w_kv_b[l], (1, 0, 2)).astype(BF16)
    w_o = w_o_attn[l].reshape(N_HEADS * V_HEAD_DIM, D_MODEL).astype(BF16)
    wc = jnp.pad(conf_dw_w[l], ((0, 1), (0, 0)))
    wc = wc.reshape(CONF_WIDTH + 1, D_CONF // LANES, LANES).transpose(1, 0, 2)
    return w_main, w_small, w_kvb, w_o, wc


def _run_group(x, seq_len, mods, row_fn, rope_tabs, caches, lw, final_g, tm, ts, tq):
    t = x.shape[0]
    b = t // seq_len
    ckv_layers, kpe_layers = [], []
    for l in range(DEPTH):
        w = lw[l]
        mod_l = mods[l]
        p, h = _inproj_call(x, mod_l, w["g1"], w["w_main"], row_fn, tm)
        outs = _qkvs_call(h, w["w_small"], w["kv_g"], rope_tabs, 512)
        qpe, ckv, kpe = outs[:3]
        ckv_layers.append(ckv)
        kpe_layers.append(kpe)
        za, uc = _mixers_call(p, w["conv_a_w"], w["wc"], w["conf_b"], w["ln_g"], w["ln_b"], seq_len, ts)
        ckv_all = ckv.reshape(b, seq_len, KV_LORA_RANK)
        if rope_tabs is None:
            kpe_all = kpe.reshape(b, seq_len, QK_ROPE_DIM)
        else:
            cache_ckv, cache_kpe = caches
            ckv_all = jnp.concatenate([ckv_all, cache_ckv[:, l]], axis=1)
            kpe_all = jnp.concatenate([outs[3].reshape(b, seq_len, QK_ROPE_DIM), cache_kpe[:, l]], axis=1)
        sk = ckv_all.shape[1]
        k, v = _kvup_call(ckv_all, kpe_all, w["w_kvb"], 512 if sk % 512 == 0 else sk)
        o = _attn_call(p, qpe, k, v, seq_len, tq)
        x = _merge_call(za, o, uc, p, x, mod_l, w["w_out_a"], w["w_o"], w["w_out_c"], w["w_merge"], row_fn, 512)
        x = _ffn_call(x, mod_l, w["g2"], w["w_ff1"], w["w_ff2"],
                      final_g if l == DEPTH - 1 else None, row_fn, 512)
    return x, ckv_layers, kpe_layers


def kernel(x_prompt, x_sample, cache_ckv, cache_kpe, c, c_ctx, w_mod, b_mod, norm1_g, w_in, conv_a_w, w_out_a, kv_norm_g, w_kv_b, w_o_attn, conf_dw_w, conf_dw_b, conf_ln_g, conf_ln_b, w_out_c, w_merge, norm2_g, w_ff1, w_ff2, final_norm_g):
    batch, seq, _ = x_prompt.shape
    dec_batch, dec_seq, _ = x_sample.shape

    n_rows = 8
    c_rows = jnp.concatenate(
        [c_ctx[None, :], c, jnp.zeros((n_rows - 1 - dec_batch, D_MODEL), F32)], axis=0)
    mod = _mod_call(c_rows, w_mod, b_mod).reshape(DEPTH, n_rows, 6, D_MODEL)
    mods = [mod[l] for l in range(DEPTH)]

    lw = []
    for l in range(DEPTH):
        w_main, w_small, w_kvb, w_o, wc = _prep_layer_weights(l, w_in, w_kv_b, w_o_attn, conf_dw_w)
        lw.append(dict(
            g1=norm1_g[l][None, :], g2=norm2_g[l][None, :], kv_g=kv_norm_g[l][None, :],
            w_main=w_main, w_small=w_small, w_kvb=w_kvb, w_o=w_o, wc=wc,
            conv_a_w=conv_a_w[l], conf_b=conf_dw_b[l][None, :],
            ln_g=conf_ln_g[l][None, :], ln_b=conf_ln_b[l][None, :],
            w_out_a=w_out_a[l].astype(BF16), w_out_c=w_out_c[l].astype(BF16),
            w_merge=w_merge[l].astype(BF16),
            w_ff1=w_ff1[l].astype(BF16), w_ff2=w_ff2[l].astype(BF16)))
    final_g = final_norm_g[None, :]

    tm = 1024
    yp, ckv_layers, kpe_layers = _run_group(
        x_prompt.reshape(batch * seq, D_MODEL), seq, mods, lambda tok: 0, None, None,
        lw, final_g, tm, seq, seq)
    ys, _, _ = _run_group(
        x_sample.reshape(dec_batch * dec_seq, D_MODEL), dec_seq, mods,
        lambda tok: 1 + tok // dec_seq, _rope_tables(dec_seq), (cache_ckv, cache_kpe),
        lw, final_g, tm, 256, 256)

    new_ckv = jnp.stack([a.reshape(batch, seq, KV_LORA_RANK) for a in ckv_layers], axis=1)
    new_kpe = jnp.stack([a.reshape(batch, seq, QK_ROPE_DIM) for a in kpe_layers], axis=1)
    return (yp.reshape(batch, seq, D_MODEL), ys.reshape(dec_batch, dec_seq, D_MODEL), new_ckv, new_kpe)
```

```python
import functools

import jax
import jax.numpy as jnp
from jax import lax
from jax.experimental import pallas as pl
from jax.experimental.pallas import tpu as pltpu

F32 = jnp.float32
BF16 = jnp.bfloat16

D_MODEL = 2048
DEPTH = 2
GRID_W = 64
D_CONV_A = 1024
CONV_A_WIDTH = 3
N_HEADS = 16
QK_NOPE_DIM = 128
QK_ROPE_DIM = 64
V_HEAD_DIM = 128
KV_LORA_RANK = 512
QK_HEAD_DIM = QK_NOPE_DIM + QK_ROPE_DIM
N_FREQ = QK_ROPE_DIM // 4
ROPE_BASE = 10000.0
ATTN_SCALE = QK_HEAD_DIM ** -0.5
SCORE_SCALE = ATTN_SCALE * 1.4426950408889634
D_CONF = 1024
CONF_WIDTH = 31
D_FF = 4 * D_MODEL
NORM_EPS = 1e-6

SPLIT_A = 3 * D_CONV_A
SPLIT_Q = N_HEADS * QK_HEAD_DIM
SPLIT_KV = KV_LORA_RANK + QK_ROPE_DIM
SPLIT_CONF = 2 * D_CONF
OFF_Q = SPLIT_A
OFF_KV = OFF_Q + SPLIT_Q
OFF_CONF = OFF_KV + SPLIT_KV
OFF_GATE = OFF_CONF + SPLIT_CONF

LANES = 128
BF16_ROWS = 16
MIB = 1024 * 1024

P_OFF_A = 0
P_OFF_QN = SPLIT_A
P_OFF_CONF = P_OFF_QN + N_HEADS * QK_NOPE_DIM
P_OFF_GATE = P_OFF_CONF + SPLIT_CONF
P_WIDTH = P_OFF_GATE + 3 * D_MODEL
S_OFF_CKV = N_HEADS * QK_ROPE_DIM
S_OFF_KPE = S_OFF_CKV + KV_LORA_RANK
S_WIDTH = S_OFF_KPE + LANES
KEY_WIDTH = 2 * LANES

ROW_CHUNK = 16
CONV_HALO = 16


def _cparams(n_axes, vmem_mib):
    return pltpu.CompilerParams(dimension_semantics=("arbitrary",) * n_axes,
                                vmem_limit_bytes=vmem_mib * MIB)


def _mod_kernel(c_ref, w_ref, b_ref, o_ref):
    c = c_ref[...]
    s = c * jax.nn.sigmoid(c)
    o_ref[...] = jnp.dot(s.astype(BF16), w_ref[...].astype(BF16),
                         preferred_element_type=F32) + b_ref[...]


def _mod_call(c_rows, w_mod, b_mod):
    n_rows = c_rows.shape[0]
    n = 6 * D_MODEL
    tn = 1024
    return pl.pallas_call(
        _mod_kernel,
        grid=(DEPTH, n // tn),
        in_specs=[pl.BlockSpec((n_rows, D_MODEL), lambda l, j: (0, 0)),
                  pl.BlockSpec((None, D_MODEL, tn), lambda l, j: (l, 0, j)),
                  pl.BlockSpec((None, 1, tn), lambda l, j: (l, 0, j))],
        out_specs=pl.BlockSpec((None, n_rows, tn), lambda l, j: (l, 0, j)),
        out_shape=jax.ShapeDtypeStruct((DEPTH, n_rows, n), F32),
        compiler_params=_cparams(2, 40),
        name="mod",
    )(c_rows, w_mod, b_mod.reshape(DEPTH, 1, n))


def _modulated_rmsnorm_rows(x, g, scale, shift):
    ms = jnp.mean(x * x, axis=-1, keepdims=True)
    y = x * lax.rsqrt(ms + NORM_EPS) * g
    return y * (1.0 + scale) + shift


def _rope_lanes(x, cos, s_up, s_dn):
    up = pltpu.roll(x, LANES - N_FREQ, 1)
    dn = pltpu.roll(x, N_FREQ, 1)
    return x * cos + up * s_up + dn * s_dn


def _inproj_kernel(x_ref, mod_ref, g_ref, w_ref, p_ref, h_ref):
    @pl.when(pl.program_id(1) == 0)
    def _():
        g = g_ref[...]
        shift = mod_ref[0:1, :]
        scale = mod_ref[1:2, :]

        def body(r, carry):
            rows = pl.ds(pl.multiple_of(r * ROW_CHUNK, ROW_CHUNK), ROW_CHUNK)
            h = _modulated_rmsnorm_rows(x_ref[rows, :], g, scale, shift)
            h_ref[rows, :] = h.astype(BF16)
            return carry

        lax.fori_loop(0, x_ref.shape[0] // ROW_CHUNK, body, 0)

    p_ref[...] = jnp.dot(h_ref[...], w_ref[...], preferred_element_type=F32).astype(BF16)


def _inproj_call(x, mod_l, g1, w_main, row_fn, tm):
    t = x.shape[0]
    tn = 1024
    return pl.pallas_call(
        _inproj_kernel,
        grid=(t // tm, P_WIDTH // tn),
        in_specs=[pl.BlockSpec((tm, D_MODEL), lambda i, j: (i, 0)),
                  pl.BlockSpec((None, 6, D_MODEL), lambda i, j: (row_fn(i * tm), 0, 0)),
                  pl.BlockSpec((1, D_MODEL), lambda i, j: (0, 0)),
                  pl.BlockSpec((D_MODEL, tn), lambda i, j: (0, j))],
        out_specs=[pl.BlockSpec((tm, tn), lambda i, j: (i, j)),
                   pl.BlockSpec((tm, D_MODEL), lambda i, j: (i, 0))],
        out_shape=[jax.ShapeDtypeStruct((t, P_WIDTH), BF16),
                   jax.ShapeDtypeStruct((t, D_MODEL), BF16)],
        compiler_params=_cparams(2, 48),
        name="inproj",
    )(x, mod_l, g1, w_main)


def _qkvs_kernel(*refs, rope):
    if rope:
        h_ref, w_ref, kvg_ref, cos_ref, sup_ref, sdn_ref, qpe_ref, ckv_ref, kpe_ref, kpr_ref = refs
    else:
        h_ref, w_ref, kvg_ref, qpe_ref, ckv_ref, kpe_ref = refs
    acc = jnp.dot(h_ref[...], w_ref[...], preferred_element_type=F32)
    if rope:
        cos, s_up, s_dn = cos_ref[...], sup_ref[...], sdn_ref[...]
    for c in range(S_OFF_CKV // LANES):
        q = acc[:, c * LANES:(c + 1) * LANES]
        if rope:
            q = _rope_lanes(q, cos, s_up, s_dn)
        qpe_ref[:, c * LANES:(c + 1) * LANES] = q.astype(BF16)
    ckv = acc[:, S_OFF_CKV:S_OFF_KPE]
    ms = jnp.mean(ckv * ckv, axis=-1, keepdims=True)
    ckv_ref[...] = ckv * lax.rsqrt(ms + NORM_EPS) * kvg_ref[...]
    kpe2 = acc[:, S_OFF_KPE:S_WIDTH]
    kpe_ref[...] = kpe2[:, :QK_ROPE_DIM]
    if rope:
        kpr_ref[...] = _rope_lanes(kpe2, cos, s_up, s_dn)[:, :QK_ROPE_DIM]


def _qkvs_call(h, w_small, kv_g, rope_tabs, tm):
    t = h.shape[0]
    rope = rope_tabs is not None
    in_specs = [pl.BlockSpec((tm, D_MODEL), lambda i: (i, 0)),
                pl.BlockSpec((D_MODEL, S_WIDTH), lambda i: (0, 0)),
                pl.BlockSpec((1, KV_LORA_RANK), lambda i: (0, 0))]
    args = [h, w_small, kv_g]
    out_specs = [pl.BlockSpec((tm, S_OFF_CKV), lambda i: (i, 0)),
                 pl.BlockSpec((tm, KV_LORA_RANK), lambda i: (i, 0)),
                 pl.BlockSpec((tm, QK_ROPE_DIM), lambda i: (i, 0))]
    out_shape = [jax.ShapeDtypeStruct((t, S_OFF_CKV), BF16),
                 jax.ShapeDtypeStruct((t, KV_LORA_RANK), F32),
                 jax.ShapeDtypeStruct((t, QK_ROPE_DIM), F32)]
    if rope:
        seq_tiles = rope_tabs[0].shape[0] // tm
        in_specs += [pl.BlockSpec((tm, LANES), lambda i: (i % seq_tiles, 0))] * 3
        args += list(rope_tabs)
        out_specs.append(pl.BlockSpec((tm, QK_ROPE_DIM), lambda i: (i, 0)))
        out_shape.append(jax.ShapeDtypeStruct((t, QK_ROPE_DIM), F32))
    return pl.pallas_call(
        functools.partial(_qkvs_kernel, rope=rope),
        grid=(t // tm,),
        in_specs=in_specs, out_specs=out_specs, out_shape=out_shape,
        compiler_params=_cparams(1, 40),
        name="qkvs",
    )(*args)


def _mixers_kernel(xin, gb, gc, ca, cb, xin_p, gc_p, ca_p, cb_p, xin_n, gc_n, ca_n, cb_n,
                   wa_ref, wc_ref, cbias_ref, lng_ref, lnb_ref, za_ref, uc_ref,
                   bufa, bufc, convc, *, ts, tiles_per_seq):
    i = pl.program_id(0)
    first = (i % tiles_per_seq) == 0
    last = (i % tiles_per_seq) == tiles_per_seq - 1
    hl = CONV_HALO

    def f32(r):
        return r[...].astype(F32)

    bufa[0:hl, :] = jnp.where(first, 0.0, f32(gc_p) * f32(xin_p))
    bufa[hl:hl + ts, :] = f32(gc) * f32(xin)
    bufa[hl + ts:, :] = jnp.where(last, 0.0, f32(gc_n) * f32(xin_n))
    pad_a = (CONV_A_WIDTH - 1) // 2
    y = jnp.zeros((ts, D_CONV_A), F32)
    for k in range(CONV_A_WIDTH):
        y = y + wa_ref[k:k + 1, :] * bufa[pl.ds(hl - pad_a + k, ts), :]
    za_ref[...] = (f32(gb) * y).astype(BF16)

    def glu(a, b):
        return f32(a) * jax.nn.sigmoid(f32(b))

    n_chunks = D_CONF // LANES
    up = jnp.where(first, 0.0, glu(ca_p, cb_p))
    um = glu(ca, cb)
    un = jnp.where(last, 0.0, glu(ca_n, cb_n))
    for c in range(n_chunks):
        lanes = slice(c * LANES, (c + 1) * LANES)
        bufc[c, 0:hl, :] = up[:, lanes]
        bufc[c, hl:hl + ts, :] = um[:, lanes]
        bufc[c, hl + ts:, :] = un[:, lanes]
    pad_c = (CONF_WIDTH - 1) // 2

    def chunk_body(c, carry):
        w = wc_ref[c]
        acc = jnp.zeros((ts, LANES), F32)
        for k in range(CONF_WIDTH):
            acc = acc + w[k:k + 1, :] * bufc[c, pl.ds(hl - pad_c + k, ts), :]
        convc[c] = acc
        return carry

    lax.fori_loop(0, n_chunks, chunk_body, 0)
    u = jnp.concatenate([convc[c] for c in range(n_chunks)], axis=1) + cbias_ref[...]
    mu = jnp.mean(u, axis=-1, keepdims=True)
    uc = u - mu
    var = jnp.mean(uc * uc, axis=-1, keepdims=True)
    v = uc * lax.rsqrt(var + NORM_EPS) * lng_ref[...] + lnb_ref[...]
    uc_ref[...] = (v * jax.nn.sigmoid(v)).astype(BF16)


def _mixers_call(p, conv_a_w, conf_w_chunks, conf_b, ln_g, ln_b, seq_len, ts):
    t = p.shape[0]
    tiles_per_seq = seq_len // ts
    halo_per_tile = ts // CONV_HALO
    n_halo_blocks = t // CONV_HALO
    col_a = P_OFF_A // D_CONV_A
    col_c = P_OFF_CONF // D_CONF
    cols = [col_a, col_a + 1, col_a + 2, col_c, col_c + 1]
    halo_cols = [col_a, col_a + 2, col_c, col_c + 1]

    def main_spec(col):
        return pl.BlockSpec((ts, D_CONF), lambda i: (i, col))

    def prev_spec(col):
        return pl.BlockSpec((CONV_HALO, D_CONF),
                            lambda i: (jnp.maximum(i * halo_per_tile - 1, 0), col))

    def next_spec(col):
        return pl.BlockSpec((CONV_HALO, D_CONF),
                            lambda i: (jnp.minimum((i + 1) * halo_per_tile, n_halo_blocks - 1), col))

    def full(a):
        return pl.BlockSpec(a.shape, lambda i: (0,) * a.ndim)

    small = [conv_a_w, conf_w_chunks, conf_b, ln_g, ln_b]
    in_specs = ([main_spec(c) for c in cols] + [prev_spec(c) for c in halo_cols]
                + [next_spec(c) for c in halo_cols] + [full(a) for a in small])
    return pl.pallas_call(
        functools.partial(_mixers_kernel, ts=ts, tiles_per_seq=tiles_per_seq),
        grid=(t // ts,),
        in_specs=in_specs,
        out_specs=[pl.BlockSpec((ts, D_CONV_A), lambda i: (i, 0)),
                   pl.BlockSpec((ts, D_CONF), lambda i: (i, 0))],
        out_shape=[jax.ShapeDtypeStruct((t, D_CONV_A), BF16),
                   jax.ShapeDtypeStruct((t, D_CONF), BF16)],
        scratch_shapes=[pltpu.VMEM((ts + 2 * CONV_HALO, D_CONV_A), F32),
                        pltpu.VMEM((D_CONF // LANES, ts + 2 * CONV_HALO, LANES), F32),
                        pltpu.VMEM((D_CONF // LANES, ts, LANES), F32)],
        compiler_params=_cparams(1, 40),
        name="mixers",
    )(*([p] * 13), *small)


def _kvup_kernel(ckv_ref, kpe_ref, wk_ref, wvt_ref, k_ref, vt_ref):
    ckv = ckv_ref[...].astype(BF16)
    kpe = (kpe_ref[...] * SCORE_SCALE).astype(BF16)
    zeros = jnp.zeros_like(kpe)
    for h in range(N_HEADS):
        kn = jnp.dot(ckv, wk_ref[h], preferred_element_type=F32)
        k_ref[h, :, 0:QK_NOPE_DIM] = (kn * SCORE_SCALE).astype(BF16)
        lo, hi = (kpe, zeros) if h % 2 == 0 else (zeros, kpe)
        k_ref[h, :, QK_NOPE_DIM:QK_NOPE_DIM + QK_ROPE_DIM] = lo
        k_ref[h, :, QK_NOPE_DIM + QK_ROPE_DIM:] = hi
        vt = lax.dot_general(wvt_ref[h], ckv, (((1,), (1,)), ((), ())), preferred_element_type=F32)
        vt_ref[h] = vt.astype(BF16)


def _kvup_call(ckv_all, kpe_all, w_k, w_vt, tk):
    b, sk, _ = ckv_all.shape
    return pl.pallas_call(
        _kvup_kernel,
        grid=(b, sk // tk),
        in_specs=[pl.BlockSpec((None, tk, KV_LORA_RANK), lambda bi, s: (bi, s, 0)),
                  pl.BlockSpec((None, tk, QK_ROPE_DIM), lambda bi, s: (bi, s, 0)),
                  pl.BlockSpec(w_k.shape, lambda bi, s: (0, 0, 0)),
                  pl.BlockSpec(w_vt.shape, lambda bi, s: (0, 0, 0))],
        out_specs=[pl.BlockSpec((None, N_HEADS, tk, KEY_WIDTH), lambda bi, s: (bi, 0, s, 0)),
                   pl.BlockSpec((None, N_HEADS, V_HEAD_DIM, tk), lambda bi, s: (bi, 0, 0, s))],
        out_shape=[jax.ShapeDtypeStruct((b, N_HEADS, sk, KEY_WIDTH), BF16),
                   jax.ShapeDtypeStruct((b, N_HEADS, V_HEAD_DIM, sk), BF16)],
        compiler_params=_cparams(2, 40),
        name="kvup",
    )(ckv_all, kpe_all, w_k, w_vt)


def _attn_kernel(qn_ref, qp_ref, k_ref, vt_ref, o_ref):
    qp = qp_ref[...]
    for h in range(2):
        q = jnp.concatenate([qn_ref[:, h * QK_NOPE_DIM:(h + 1) * QK_NOPE_DIM], qp], axis=1)
        st = lax.dot_general(k_ref[h], q, (((1,), (1,)), ((), ())), preferred_element_type=F32)
        m = jnp.max(st, axis=0, keepdims=True)
        p = jnp.exp2(st - m)
        l = jnp.sum(p, axis=0, keepdims=True)
        ot = jnp.dot(vt_ref[h], p.astype(BF16), preferred_element_type=F32)
        o_ref[:, h * V_HEAD_DIM:(h + 1) * V_HEAD_DIM] = (ot * (1.0 / l)).T.astype(BF16)


def _attn_pipe_kernel(qn_ref, qp_ref, k_ref, vt_ref, o_ref, s_scr, m_scr, *, ck):
    sk = k_ref.shape[1]

    @pl.when(pl.program_id(2) == 0)
    def _():
        s_scr[...] = jnp.zeros_like(s_scr)
        m_scr[...] = jnp.zeros_like(m_scr)

    qp = qp_ref[...]
    for h in range(2):
        q = jnp.concatenate([qn_ref[:, h * QK_NOPE_DIM:(h + 1) * QK_NOPE_DIM], qp], axis=1)
        m_old = m_scr[h, 0:1, :]
        m_new = None
        l = None
        acc = None
        for c in range(sk // ck):
            rows = pl.ds(c * ck, ck)
            p = jnp.exp2(s_scr[h, rows, :] - m_old)
            lc = jnp.sum(p, axis=0, keepdims=True)
            pv = jnp.dot(vt_ref[h, :, rows], p.astype(BF16), preferred_element_type=F32)
            st = lax.dot_general(k_ref[h, rows, :], q, (((1,), (1,)), ((), ())),
                                 preferred_element_type=F32)
            s_scr[h, rows, :] = st
            mc = jnp.max(st, axis=0, keepdims=True)
            l = lc if l is None else l + lc
            acc = pv if acc is None else acc + pv
            m_new = mc if m_new is None else jnp.maximum(m_new, mc)
        m_scr[h] = jnp.broadcast_to(m_new, m_scr.shape[1:])
        o_ref[:, h * V_HEAD_DIM:(h + 1) * V_HEAD_DIM] = (acc * (1.0 / l)).T.astype(BF16)


def _attn_pipe_call(p, qpe, k, vt, seq_len, tq):
    t = p.shape[0]
    sk = k.shape[2]
    q_tiles = seq_len // tq
    qn_col0 = P_OFF_QN // (2 * QK_NOPE_DIM)

    def q_row(b, qi):
        return b * q_tiles + jnp.minimum(qi, q_tiles - 1)

    def o_row(b, qi):
        return b * q_tiles + jnp.maximum(qi - 1, 0)

    return pl.pallas_call(
        functools.partial(_attn_pipe_kernel, ck=256),
        grid=(t // seq_len, N_HEADS // 2, q_tiles + 1),
        in_specs=[pl.BlockSpec((tq, 2 * QK_NOPE_DIM), lambda b, hp, qi: (q_row(b, qi), qn_col0 + hp)),
                  pl.BlockSpec((tq, 2 * QK_ROPE_DIM), lambda b, hp, qi: (q_row(b, qi), hp)),
                  pl.BlockSpec((None, 2, sk, KEY_WIDTH), lambda b, hp, qi: (b, hp, 0, 0)),
                  pl.BlockSpec((None, 2, V_HEAD_DIM, sk), lambda b, hp, qi: (b, hp, 0, 0))],
        out_specs=pl.BlockSpec((tq, 2 * V_HEAD_DIM), lambda b, hp, qi: (o_row(b, qi), hp)),
        out_shape=jax.ShapeDtypeStruct((t, N_HEADS * V_HEAD_DIM), BF16),
        scratch_shapes=[pltpu.VMEM((2, sk, tq), F32), pltpu.VMEM((2, 8, tq), F32)],
        compiler_params=_cparams(3, 48),
        name="attn_pipe",
    )(p, qpe, k, vt)


def _attn_call(p, qpe, k, vt, seq_len, tq):
    t = p.shape[0]
    sk = k.shape[2]
    q_tiles = seq_len // tq
    if q_tiles > 1:
        return _attn_pipe_call(p, qpe, k, vt, seq_len, tq)
    qn_col0 = P_OFF_QN // (2 * QK_NOPE_DIM)
    return pl.pallas_call(
        _attn_kernel,
        grid=(t // seq_len, N_HEADS // 2, q_tiles),
        in_specs=[pl.BlockSpec((tq, 2 * QK_NOPE_DIM), lambda b, hp, qi: (b * q_tiles + qi, qn_col0 + hp)),
                  pl.BlockSpec((tq, 2 * QK_ROPE_DIM), lambda b, hp, qi: (b * q_tiles + qi, hp)),
                  pl.BlockSpec((None, 2, sk, KEY_WIDTH), lambda b, hp, qi: (b, hp, 0, 0)),
                  pl.BlockSpec((None, 2, V_HEAD_DIM, sk), lambda b, hp, qi: (b, hp, 0, 0))],
        out_specs=pl.BlockSpec((tq, 2 * V_HEAD_DIM), lambda b, hp, qi: (b * q_tiles + qi, hp)),
        out_shape=jax.ShapeDtypeStruct((t, N_HEADS * V_HEAD_DIM), BF16),
        compiler_params=_cparams(3, 48),
        name="attn",
    )(p, qpe, k, vt)


def _merge_kernel(za_ref, o_ref, uc_ref, g0_ref, g1_ref, g2_ref, x_ref, mod_ref,
                  woa_ref, wo_ref, woc_ref, wm_ref, out_ref):
    j = pl.program_id(1)

    def sig(r):
        return jax.nn.sigmoid(r[...].astype(F32))

    ya = jnp.dot(za_ref[...], woa_ref[...], preferred_element_type=F32)
    yb = jnp.dot(o_ref[...], wo_ref[...], preferred_element_type=F32)
    yc = jnp.dot(uc_ref[...], woc_ref[...], preferred_element_type=F32)
    merged = sig(g0_ref) * ya + sig(g1_ref) * yb + sig(g2_ref) * yc
    contrib = jnp.dot(merged.astype(BF16), wm_ref[...], preferred_element_type=F32)

    @pl.when(j == 0)
    def _():
        out_ref[...] = contrib

    @pl.when(j > 0)
    def _():
        out_ref[...] += contrib

    @pl.when(j == pl.num_programs(1) - 1)
    def _():
        gate1 = mod_ref[2:3, :]

        def body(r, carry):
            rows = pl.ds(pl.multiple_of(r * ROW_CHUNK, ROW_CHUNK), ROW_CHUNK)
            out_ref[rows, :] = x_ref[rows, :] + gate1 * out_ref[rows, :]
            return carry

        lax.fori_loop(0, out_ref.shape[0] // ROW_CHUNK, body, 0)


def _merge_call(za, o, uc, p, x, mod_l, w_out_a, w_o, w_out_c, w_merge, row_fn, tm):
    t = x.shape[0]
    tn = 512
    gate_col0 = P_OFF_GATE // tn
    per_gate = D_MODEL // tn

    def gate_spec(g):
        return pl.BlockSpec((tm, tn), lambda i, j: (i, gate_col0 + g * per_gate + j))

    return pl.pallas_call(
        _merge_kernel,
        grid=(t // tm, D_MODEL // tn),
        in_specs=[pl.BlockSpec((tm, D_CONV_A), lambda i, j: (i, 0)),
                  pl.BlockSpec((tm, D_MODEL), lambda i, j: (i, 0)),
                  pl.BlockSpec((tm, D_CONF), lambda i, j: (i, 0)),
                  gate_spec(0), gate_spec(1), gate_spec(2),
                  pl.BlockSpec((tm, D_MODEL), lambda i, j: (i, 0)),
                  pl.BlockSpec((None, 6, D_MODEL), lambda i, j: (row_fn(i * tm), 0, 0)),
                  pl.BlockSpec((D_CONV_A, tn), lambda i, j: (0, j)),
                  pl.BlockSpec((D_MODEL, tn), lambda i, j: (0, j)),
                  pl.BlockSpec((D_CONF, tn), lambda i, j: (0, j)),
                  pl.BlockSpec((tn, D_MODEL), lambda i, j: (j, 0))],
        out_specs=pl.BlockSpec((tm, D_MODEL), lambda i, j: (i, 0)),
        out_shape=jax.ShapeDtypeStruct((t, D_MODEL), F32),
        compiler_params=_cparams(2, 52),
        name="merge",
    )(za, o, uc, p, p, p, x, mod_l, w_out_a, w_o, w_out_c, w_merge)


def _ffn_kernel(*refs, final_norm):
    if final_norm:
        x_ref, mod_ref, g_ref, w1_ref, w2_ref, fg_ref, out_ref, h_ref = refs
    else:
        x_ref, mod_ref, g_ref, w1_ref, w2_ref, out_ref, h_ref = refs
    k = pl.program_id(1)
    n_chunks = x_ref.shape[0] // ROW_CHUNK

    @pl.when(k == 0)
    def _():
        g = g_ref[...]
        shift = mod_ref[3:4, :]
        scale = mod_ref[4:5, :]

        def body(r, carry):
            rows = pl.ds(pl.multiple_of(r * ROW_CHUNK, ROW_CHUNK), ROW_CHUNK)
            h_ref[rows, :] = _modulated_rmsnorm_rows(x_ref[rows, :], g, scale, shift).astype(BF16)
            return carry

        lax.fori_loop(0, n_chunks, body, 0)

    a = jnp.dot(h_ref[...], w1_ref[...], preferred_element_type=F32)
    a = jnp.square(jnp.maximum(a, 0.0)).astype(BF16)
    contrib = jnp.dot(a, w2_ref[...], preferred_element_type=F32)

    @pl.when(k == 0)
    def _():
        out_ref[...] = contrib

    @pl.when(k > 0)
    def _():
        out_ref[...] += contrib

    @pl.when(k == pl.num_programs(1) - 1)
    def _():
        gate2 = mod_ref[5:6, :]

        def body(r, carry):
            rows = pl.ds(pl.multiple_of(r * ROW_CHUNK, ROW_CHUNK), ROW_CHUNK)
            y = x_ref[rows, :] + gate2 * out_ref[rows, :]
            if final_norm:
                ms = jnp.mean(y * y, axis=-1, keepdims=True)
                y = y * lax.rsqrt(ms + NORM_EPS) * fg_ref[...]
            out_ref[rows, :] = y
            return carry

        lax.fori_loop(0, n_chunks, body, 0)


def _ffn_call(x, mod_l, g2, w1, w2, final_g, row_fn, tm):
    t = x.shape[0]
    tf = 512
    final_norm = final_g is not None
    in_specs = [pl.BlockSpec((tm, D_MODEL), lambda i, k: (i, 0)),
                pl.BlockSpec((None, 6, D_MODEL), lambda i, k: (row_fn(i * tm), 0, 0)),
                pl.BlockSpec((1, D_MODEL), lambda i, k: (0, 0)),
                pl.BlockSpec((D_MODEL, tf), lambda i, k: (0, k)),
                pl.BlockSpec((tf, D_MODEL), lambda i, k: (k, 0))]
    args = [x, mod_l, g2, w1, w2]
    if final_norm:
        in_specs.append(pl.BlockSpec((1, D_MODEL), lambda i, k: (0, 0)))
        args.append(final_g)
    return pl.pallas_call(
        functools.partial(_ffn_kernel, final_norm=final_norm),
        grid=(t // tm, D_FF // tf),
        in_specs=in_specs,
        out_specs=pl.BlockSpec((tm, D_MODEL), lambda i, k: (i, 0)),
        out_shape=jax.ShapeDtypeStruct((t, D_MODEL), F32),
        scratch_shapes=[pltpu.VMEM((tm, D_MODEL), BF16)],
        compiler_params=_cparams(2, 52),
        name="ffn",
    )(*args)


def _rope_tables(n_tokens):
    pos = jnp.arange(n_tokens)
    row = (pos // GRID_W).astype(F32)
    col = (pos % GRID_W).astype(F32)
    inv_freq = ROPE_BASE ** (-jnp.arange(N_FREQ, dtype=F32) / N_FREQ)
    ang = jnp.stack([row[:, None] * inv_freq, col[:, None] * inv_freq], axis=1)
    cos, sin = jnp.cos(ang), jnp.sin(ang)
    zero = jnp.zeros_like(sin)
    cos64 = jnp.stack([cos, cos], axis=2).reshape(n_tokens, QK_ROPE_DIM)
    up64 = jnp.stack([-sin, zero], axis=2).reshape(n_tokens, QK_ROPE_DIM)
    dn64 = jnp.stack([zero, sin], axis=2).reshape(n_tokens, QK_ROPE_DIM)
    return tuple(jnp.concatenate([a, a], axis=1) for a in (cos64, up64, dn64))


def _prep_layer_weights(l, w_in, w_kv_b, w_o_attn, conf_dw_w):
    w = w_in[l]
    w_q = w[:, OFF_Q:OFF_KV].reshape(D_MODEL, N_HEADS, QK_HEAD_DIM)
    w_qn = w_q[:, :, :QK_NOPE_DIM].reshape(D_MODEL, N_HEADS * QK_NOPE_DIM)
    w_qp = w_q[:, :, QK_NOPE_DIM:].reshape(D_MODEL, N_HEADS * QK_ROPE_DIM)
    w_main = jnp.concatenate([w[:, :OFF_Q], w_qn, w[:, OFF_CONF:]], axis=1).astype(BF16)
    w_small = jnp.concatenate(
        [w_qp, w[:, OFF_KV:OFF_CONF], jnp.zeros((D_MODEL, LANES - QK_ROPE_DIM), F32)], axis=1).astype(BF16)
    w_k = jnp.transpose(w_kv_b[l][:, :, :QK_NOPE_DIM], (1, 0, 2)).astype(BF16)
    w_vt = jnp.transpose(w_kv_b[l][:, :, QK_NOPE_DIM:], (1, 2, 0)).astype(BF16)
    w_o = w_o_attn[l].reshape(N_HEADS * V_HEAD_DIM, D_MODEL).astype(BF16)
    wc = jnp.pad(conf_dw_w[l], ((0, 1), (0, 0)))
    wc = wc.reshape(CONF_WIDTH + 1, D_CONF // LANES, LANES).transpose(1, 0, 2)
    return w_main, w_small, w_k, w_vt, w_o, wc


def _run_group(x, seq_len, mods, row_fn, rope_tabs, caches, lw, final_g, tm, ts, tq):
    t = x.shape[0]
    b = t // seq_len
    ckv_layers, kpe_layers = [], []
    for l in range(DEPTH):
        w = lw[l]
        mod_l = mods[l]
        p, h = _inproj_call(x, mod_l, w["g1"], w["w_main"], row_fn, tm)
        outs = _qkvs_call(h, w["w_small"], w["kv_g"], rope_tabs, 512)
        qpe, ckv, kpe = outs[:3]
        ckv_layers.append(ckv)
        kpe_layers.append(kpe)
        za, uc = _mixers_call(p, w["conv_a_w"], w["wc"], w["conf_b"], w["ln_g"], w["ln_b"], seq_len, ts)
        ckv_all = ckv.reshape(b, seq_len, KV_LORA_RANK)
        if rope_tabs is None:
            kpe_all = kpe.reshape(b, seq_len, QK_ROPE_DIM)
        else:
            cache_ckv, cache_kpe = caches
            ckv_all = jnp.concatenate([ckv_all, cache_ckv[:, l]], axis=1)
            kpe_all = jnp.concatenate([outs[3].reshape(b, seq_len, QK_ROPE_DIM), cache_kpe[:, l]], axis=1)
        sk = ckv_all.shape[1]
        k, vt = _kvup_call(ckv_all, kpe_all, w["w_k"], w["w_vt"], 512 if sk % 512 == 0 else sk)
        o = _attn_call(p, qpe, k, vt, seq_len, tq)
        x = _merge_call(za, o, uc, p, x, mod_l, w["w_out_a"], w["w_o"], w["w_out_c"], w["w_merge"], row_fn, 512)
        x = _ffn_call(x, mod_l, w["g2"], w["w_ff1"], w["w_ff2"],
                      final_g if l == DEPTH - 1 else None, row_fn, 512)
    return x, ckv_layers, kpe_layers


def kernel(x_prompt, x_sample, cache_ckv, cache_kpe, c, c_ctx, w_mod, b_mod, norm1_g, w_in, conv_a_w, w_out_a, kv_norm_g, w_kv_b, w_o_attn, conf_dw_w, conf_dw_b, conf_ln_g, conf_ln_b, w_out_c, w_merge, norm2_g, w_ff1, w_ff2, final_norm_g):
    batch, seq, _ = x_prompt.shape
    dec_batch, dec_seq, _ = x_sample.shape

    n_rows = 8
    c_rows = jnp.concatenate(
        [c_ctx[None, :], c, jnp.zeros((n_rows - 1 - dec_batch, D_MODEL), F32)], axis=0)
    mod = _mod_call(c_rows, w_mod, b_mod).reshape(DEPTH, n_rows, 6, D_MODEL)
    mods = [mod[l] for l in range(DEPTH)]

    lw = []
    for l in range(DEPTH):
        w_main, w_small, w_k, w_vt, w_o, wc = _prep_layer_weights(l, w_in, w_kv_b, w_o_attn, conf_dw_w)
        lw.append(dict(
            g1=norm1_g[l][None, :], g2=norm2_g[l][None, :], kv_g=kv_norm_g[l][None, :],
            w_main=w_main, w_small=w_small, w_k=w_k, w_vt=w_vt, w_o=w_o, wc=wc,
            conv_a_w=conv_a_w[l], conf_b=conf_dw_b[l][None, :],
            ln_g=conf_ln_g[l][None, :], ln_b=conf_ln_b[l][None, :],
            w_out_a=w_out_a[l].astype(BF16), w_out_c=w_out_c[l].astype(BF16),
            w_merge=w_merge[l].astype(BF16),
            w_ff1=w_ff1[l].astype(BF16), w_ff2=w_ff2[l].astype(BF16)))
    final_g = final_norm_g[None, :]

    tm = 1024
    yp, ckv_layers, kpe_layers = _run_group(
        x_prompt.reshape(batch * seq, D_MODEL), seq, mods, lambda tok: 0, None, None,
        lw, final_g, tm, seq, seq)
    ys, _, _ = _run_group(
        x_sample.reshape(dec_batch * dec_seq, D_MODEL), dec_seq, mods,
        lambda tok: 1 + tok // dec_seq, _rope_tables(dec_seq), (cache_ckv, cache_kpe),
        lw, final_g, tm, 256, 256)

    new_ckv = jnp.stack([a.reshape(batch, seq, KV_LORA_RANK) for a in ckv_layers], axis=1)
    new_kpe = jnp.stack([a.reshape(batch, seq, QK_ROPE_DIM) for a in kpe_layers], axis=1)
    return (yp.reshape(batch, seq, D_MODEL), ys.reshape(dec_batch, dec_seq, D_MODEL), new_ckv, new_kpe)
```

```python
import functools

import jax
import jax.numpy as jnp
from jax import lax
from jax.experimental import pallas as pl
from jax.experimental.pallas import tpu as pltpu

F32 = jnp.float32
BF16 = jnp.bfloat16

D_MODEL = 2048
DEPTH = 2
GRID_W = 64
D_CONV_A = 1024
CONV_A_WIDTH = 3
N_HEADS = 16
QK_NOPE_DIM = 128
QK_ROPE_DIM = 64
V_HEAD_DIM = 128
KV_LORA_RANK = 512
QK_HEAD_DIM = QK_NOPE_DIM + QK_ROPE_DIM
N_FREQ = QK_ROPE_DIM // 4
ROPE_BASE = 10000.0
ATTN_SCALE = QK_HEAD_DIM ** -0.5
SCORE_SCALE = ATTN_SCALE * 1.4426950408889634
D_CONF = 1024
CONF_WIDTH = 31
D_FF = 4 * D_MODEL
NORM_EPS = 1e-6

SPLIT_A = 3 * D_CONV_A
SPLIT_Q = N_HEADS * QK_HEAD_DIM
SPLIT_KV = KV_LORA_RANK + QK_ROPE_DIM
SPLIT_CONF = 2 * D_CONF
OFF_Q = SPLIT_A
OFF_KV = OFF_Q + SPLIT_Q
OFF_CONF = OFF_KV + SPLIT_KV
OFF_GATE = OFF_CONF + SPLIT_CONF

LANES = 128
BF16_ROWS = 16
MIB = 1024 * 1024

P_OFF_A = 0
P_OFF_QN = SPLIT_A
P_OFF_CONF = P_OFF_QN + N_HEADS * QK_NOPE_DIM
P_OFF_GATE = P_OFF_CONF + SPLIT_CONF
P_WIDTH = P_OFF_GATE + 3 * D_MODEL
S_OFF_CKV = N_HEADS * QK_ROPE_DIM
S_OFF_KPE = S_OFF_CKV + KV_LORA_RANK
S_WIDTH = S_OFF_KPE + LANES
KEY_WIDTH = 2 * LANES

ROW_CHUNK = 16
ROW_UNROLL = 4
CONV_HALO = 16


def _cparams(n_axes, vmem_mib):
    return pltpu.CompilerParams(dimension_semantics=("arbitrary",) * n_axes,
                                vmem_limit_bytes=vmem_mib * MIB)


def _mod_kernel(c_ref, w_ref, b_ref, o_ref):
    c = c_ref[...]
    s = c * jax.nn.sigmoid(c)
    o_ref[...] = jnp.dot(s.astype(BF16), w_ref[...].astype(BF16),
                         preferred_element_type=F32) + b_ref[...]


def _mod_call(c_rows, w_mod, b_mod):
    n_rows = c_rows.shape[0]
    n = 6 * D_MODEL
    tn = 1024
    return pl.pallas_call(
        _mod_kernel,
        grid=(DEPTH, n // tn),
        in_specs=[pl.BlockSpec((n_rows, D_MODEL), lambda l, j: (0, 0)),
                  pl.BlockSpec((None, D_MODEL, tn), lambda l, j: (l, 0, j)),
                  pl.BlockSpec((None, 1, tn), lambda l, j: (l, 0, j))],
        out_specs=pl.BlockSpec((None, n_rows, tn), lambda l, j: (l, 0, j)),
        out_shape=jax.ShapeDtypeStruct((DEPTH, n_rows, n), F32),
        compiler_params=_cparams(2, 40),
        name="mod",
    )(c_rows, w_mod, b_mod.reshape(DEPTH, 1, n))


def _modulated_rmsnorm_rows(x, g, scale, shift):
    ms = jnp.mean(x * x, axis=-1, keepdims=True)
    y = x * lax.rsqrt(ms + NORM_EPS) * g
    return y * (1.0 + scale) + shift


def _rope_lanes(x, cos, s_up, s_dn):
    up = pltpu.roll(x, LANES - N_FREQ, 1)
    dn = pltpu.roll(x, N_FREQ, 1)
    return x * cos + up * s_up + dn * s_dn


def _inproj_kernel(x_ref, mod_ref, g_ref, w_ref, p_ref, h_ref):
    @pl.when(pl.program_id(1) == 0)
    def _():
        g = g_ref[...]
        shift = mod_ref[0:1, :]
        scale = mod_ref[1:2, :]

        def body(r, carry):
            rows = pl.ds(pl.multiple_of(r * ROW_CHUNK, ROW_CHUNK), ROW_CHUNK)
            h = _modulated_rmsnorm_rows(x_ref[rows, :], g, scale, shift)
            h_ref[rows, :] = h.astype(BF16)
            return carry

        lax.fori_loop(0, x_ref.shape[0] // ROW_CHUNK, body, 0, unroll=ROW_UNROLL)

    p_ref[...] = jnp.dot(h_ref[...], w_ref[...], preferred_element_type=F32).astype(BF16)


def _inproj_call(x, mod_l, g1, w_main, row_fn, tm):
    t = x.shape[0]
    tn = 1024
    return pl.pallas_call(
        _inproj_kernel,
        grid=(t // tm, P_WIDTH // tn),
        in_specs=[pl.BlockSpec((tm, D_MODEL), lambda i, j: (i, 0)),
                  pl.BlockSpec((None, 6, D_MODEL), lambda i, j: (row_fn(i * tm), 0, 0)),
                  pl.BlockSpec((1, D_MODEL), lambda i, j: (0, 0)),
                  pl.BlockSpec((D_MODEL, tn), lambda i, j: (0, j))],
        out_specs=[pl.BlockSpec((tm, tn), lambda i, j: (i, j)),
                   pl.BlockSpec((tm, D_MODEL), lambda i, j: (i, 0))],
        out_shape=[jax.ShapeDtypeStruct((t, P_WIDTH), BF16),
                   jax.ShapeDtypeStruct((t, D_MODEL), BF16)],
        compiler_params=_cparams(2, 48),
        name="inproj",
    )(x, mod_l, g1, w_main)


def _qkvs_kernel(*refs, rope):
    if rope:
        h_ref, w_ref, kvg_ref, cos_ref, sup_ref, sdn_ref, qpe_ref, ckv_ref, kpe_ref, kpr_ref = refs
    else:
        h_ref, w_ref, kvg_ref, qpe_ref, ckv_ref, kpe_ref = refs
    acc = jnp.dot(h_ref[...], w_ref[...], preferred_element_type=F32)
    if rope:
        cos, s_up, s_dn = cos_ref[...], sup_ref[...], sdn_ref[...]
    for c in range(S_OFF_CKV // LANES):
        q = acc[:, c * LANES:(c + 1) * LANES]
        if rope:
            q = _rope_lanes(q, cos, s_up, s_dn)
        qpe_ref[:, c * LANES:(c + 1) * LANES] = q.astype(BF16)
    ckv = acc[:, S_OFF_CKV:S_OFF_KPE]
    ms = jnp.mean(ckv * ckv, axis=-1, keepdims=True)
    ckv_ref[...] = ckv * lax.rsqrt(ms + NORM_EPS) * kvg_ref[...]
    kpe2 = acc[:, S_OFF_KPE:S_WIDTH]
    kpe_ref[...] = kpe2[:, :QK_ROPE_DIM]
    if rope:
        kpr_ref[...] = _rope_lanes(kpe2, cos, s_up, s_dn)[:, :QK_ROPE_DIM]


def _qkvs_call(h, w_small, kv_g, rope_tabs, tm):
    t = h.shape[0]
    rope = rope_tabs is not None
    in_specs = [pl.BlockSpec((tm, D_MODEL), lambda i: (i, 0)),
                pl.BlockSpec((D_MODEL, S_WIDTH), lambda i: (0, 0)),
                pl.BlockSpec((1, KV_LORA_RANK), lambda i: (0, 0))]
    args = [h, w_small, kv_g]
    out_specs = [pl.BlockSpec((tm, S_OFF_CKV), lambda i: (i, 0)),
                 pl.BlockSpec((tm, KV_LORA_RANK), lambda i: (i, 0)),
                 pl.BlockSpec((tm, QK_ROPE_DIM), lambda i: (i, 0))]
    out_shape = [jax.ShapeDtypeStruct((t, S_OFF_CKV), BF16),
                 jax.ShapeDtypeStruct((t, KV_LORA_RANK), F32),
                 jax.ShapeDtypeStruct((t, QK_ROPE_DIM), F32)]
    if rope:
        seq_tiles = rope_tabs[0].shape[0] // tm
        in_specs += [pl.BlockSpec((tm, LANES), lambda i: (i % seq_tiles, 0))] * 3
        args += list(rope_tabs)
        out_specs.append(pl.BlockSpec((tm, QK_ROPE_DIM), lambda i: (i, 0)))
        out_shape.append(jax.ShapeDtypeStruct((t, QK_ROPE_DIM), F32))
    return pl.pallas_call(
        functools.partial(_qkvs_kernel, rope=rope),
        grid=(t // tm,),
        in_specs=in_specs, out_specs=out_specs, out_shape=out_shape,
        compiler_params=_cparams(1, 40),
        name="qkvs",
    )(*args)


def _mixers_kernel(xin, gb, gc, ca, cb, xin_p, gc_p, ca_p, cb_p, xin_n, gc_n, ca_n, cb_n,
                   wa_ref, wc_ref, cbias_ref, lng_ref, lnb_ref, za_ref, uc_ref,
                   bufa, bufc, convc, *, ts, tiles_per_seq):
    i = pl.program_id(0)
    first = (i % tiles_per_seq) == 0
    last = (i % tiles_per_seq) == tiles_per_seq - 1
    hl = CONV_HALO

    def f32(r):
        return r[...].astype(F32)

    bufa[0:hl, :] = jnp.where(first, 0.0, f32(gc_p) * f32(xin_p))
    bufa[hl:hl + ts, :] = f32(gc) * f32(xin)
    bufa[hl + ts:, :] = jnp.where(last, 0.0, f32(gc_n) * f32(xin_n))
    pad_a = (CONV_A_WIDTH - 1) // 2
    y = jnp.zeros((ts, D_CONV_A), F32)
    for k in range(CONV_A_WIDTH):
        y = y + wa_ref[k:k + 1, :] * bufa[pl.ds(hl - pad_a + k, ts), :]
    za_ref[...] = (f32(gb) * y).astype(BF16)

    def glu(a, b):
        return f32(a) * jax.nn.sigmoid(f32(b))

    n_chunks = D_CONF // LANES
    up = jnp.where(first, 0.0, glu(ca_p, cb_p))
    um = glu(ca, cb)
    un = jnp.where(last, 0.0, glu(ca_n, cb_n))
    for c in range(n_chunks):
        lanes = slice(c * LANES, (c + 1) * LANES)
        bufc[c, 0:hl, :] = up[:, lanes]
        bufc[c, hl:hl + ts, :] = um[:, lanes]
        bufc[c, hl + ts:, :] = un[:, lanes]
    pad_c = (CONF_WIDTH - 1) // 2

    def chunk_body(c, carry):
        w = wc_ref[c]
        acc = jnp.zeros((ts, LANES), F32)
        for k in range(CONF_WIDTH):
            acc = acc + w[k:k + 1, :] * bufc[c, pl.ds(hl - pad_c + k, ts), :]
        convc[c] = acc
        return carry

    lax.fori_loop(0, n_chunks, chunk_body, 0)
    u = jnp.concatenate([convc[c] for c in range(n_chunks)], axis=1) + cbias_ref[...]
    mu = jnp.mean(u, axis=-1, keepdims=True)
    uc = u - mu
    var = jnp.mean(uc * uc, axis=-1, keepdims=True)
    v = uc * lax.rsqrt(var + NORM_EPS) * lng_ref[...] + lnb_ref[...]
    uc_ref[...] = (v * jax.nn.sigmoid(v)).astype(BF16)


def _mixers_call(p, conv_a_w, conf_w_chunks, conf_b, ln_g, ln_b, seq_len, ts):
    t = p.shape[0]
    tiles_per_seq = seq_len // ts
    halo_per_tile = ts // CONV_HALO
    n_halo_blocks = t // CONV_HALO
    col_a = P_OFF_A // D_CONV_A
    col_c = P_OFF_CONF // D_CONF
    cols = [col_a, col_a + 1, col_a + 2, col_c, col_c + 1]
    halo_cols = [col_a, col_a + 2, col_c, col_c + 1]

    def main_spec(col):
        return pl.BlockSpec((ts, D_CONF), lambda i: (i, col))

    def prev_spec(col):
        return pl.BlockSpec((CONV_HALO, D_CONF),
                            lambda i: (jnp.maximum(i * halo_per_tile - 1, 0), col))

    def next_spec(col):
        return pl.BlockSpec((CONV_HALO, D_CONF),
                            lambda i: (jnp.minimum((i + 1) * halo_per_tile, n_halo_blocks - 1), col))

    def full(a):
        return pl.BlockSpec(a.shape, lambda i: (0,) * a.ndim)

    small = [conv_a_w, conf_w_chunks, conf_b, ln_g, ln_b]
    in_specs = ([main_spec(c) for c in cols] + [prev_spec(c) for c in halo_cols]
                + [next_spec(c) for c in halo_cols] + [full(a) for a in small])
    return pl.pallas_call(
        functools.partial(_mixers_kernel, ts=ts, tiles_per_seq=tiles_per_seq),
        grid=(t // ts,),
        in_specs=in_specs,
        out_specs=[pl.BlockSpec((ts, D_CONV_A), lambda i: (i, 0)),
                   pl.BlockSpec((ts, D_CONF), lambda i: (i, 0))],
        out_shape=[jax.ShapeDtypeStruct((t, D_CONV_A), BF16),
                   jax.ShapeDtypeStruct((t, D_CONF), BF16)],
        scratch_shapes=[pltpu.VMEM((ts + 2 * CONV_HALO, D_CONV_A), F32),
                        pltpu.VMEM((D_CONF // LANES, ts + 2 * CONV_HALO, LANES), F32),
                        pltpu.VMEM((D_CONF // LANES, ts, LANES), F32)],
        compiler_params=_cparams(1, 40),
        name="mixers",
    )(*([p] * 13), *small)


def _kvup_kernel(ckv_ref, kpe_ref, wk_ref, wvt_ref, k_ref, vt_ref):
    ckv = ckv_ref[...].astype(BF16)
    kpe = (kpe_ref[...] * SCORE_SCALE).astype(BF16)
    zeros = jnp.zeros_like(kpe)
    kn = (jnp.dot(ckv, wk_ref[...], preferred_element_type=F32) * SCORE_SCALE).astype(BF16)
    vt = lax.dot_general(wvt_ref[...], ckv, (((1,), (1,)), ((), ())),
                         preferred_element_type=F32).astype(BF16)
    for h in range(N_HEADS):
        k_ref[h, :, 0:QK_NOPE_DIM] = kn[:, h * QK_NOPE_DIM:(h + 1) * QK_NOPE_DIM]
        lo, hi = (kpe, zeros) if h % 2 == 0 else (zeros, kpe)
        k_ref[h, :, QK_NOPE_DIM:QK_NOPE_DIM + QK_ROPE_DIM] = lo
        k_ref[h, :, QK_NOPE_DIM + QK_ROPE_DIM:] = hi
        vt_ref[h] = vt[h * V_HEAD_DIM:(h + 1) * V_HEAD_DIM, :]


def _kvup_call(ckv_all, kpe_all, w_k, w_vt, tk):
    b, sk, _ = ckv_all.shape
    return pl.pallas_call(
        _kvup_kernel,
        grid=(b, sk // tk),
        in_specs=[pl.BlockSpec((None, tk, KV_LORA_RANK), lambda bi, s: (bi, s, 0)),
                  pl.BlockSpec((None, tk, QK_ROPE_DIM), lambda bi, s: (bi, s, 0)),
                  pl.BlockSpec(w_k.shape, lambda bi, s: (0, 0)),
                  pl.BlockSpec(w_vt.shape, lambda bi, s: (0, 0))],
        out_specs=[pl.BlockSpec((None, N_HEADS, tk, KEY_WIDTH), lambda bi, s: (bi, 0, s, 0)),
                   pl.BlockSpec((None, N_HEADS, V_HEAD_DIM, tk), lambda bi, s: (bi, 0, 0, s))],
        out_shape=[jax.ShapeDtypeStruct((b, N_HEADS, sk, KEY_WIDTH), BF16),
                   jax.ShapeDtypeStruct((b, N_HEADS, V_HEAD_DIM, sk), BF16)],
        compiler_params=_cparams(2, 40),
        name="kvup",
    )(ckv_all, kpe_all, w_k, w_vt)


def _attn_kernel(qn_ref, qp_ref, k_ref, vt_ref, o_ref):
    for h in range(k_ref.shape[0]):
        pair = (h // 2) * LANES
        q = jnp.concatenate([qn_ref[:, h * QK_NOPE_DIM:(h + 1) * QK_NOPE_DIM],
                             qp_ref[:, pair:pair + LANES]], axis=1)
        st = lax.dot_general(k_ref[h], q, (((1,), (1,)), ((), ())), preferred_element_type=F32)
        m = jnp.max(st, axis=0, keepdims=True)
        p = jnp.exp2(st - m)
        l = jnp.sum(p, axis=0, keepdims=True)
        ot = jnp.dot(vt_ref[h], p.astype(BF16), preferred_element_type=F32)
        o_ref[:, h * V_HEAD_DIM:(h + 1) * V_HEAD_DIM] = (ot * (1.0 / l)).T.astype(BF16)


def _attn_pipe_kernel(qn_ref, qp_ref, k_ref, vt_ref, o_ref, s_scr, m_scr, *, ck):
    sk = k_ref.shape[1]

    @pl.when(pl.program_id(2) == 0)
    def _():
        s_scr[...] = jnp.zeros_like(s_scr)
        m_scr[...] = jnp.zeros_like(m_scr)

    qp = qp_ref[...]
    for h in range(2):
        q = jnp.concatenate([qn_ref[:, h * QK_NOPE_DIM:(h + 1) * QK_NOPE_DIM], qp], axis=1)
        m_old = m_scr[h, 0:1, :]
        m_new = None
        l = None
        acc = None
        for c in range(sk // ck):
            rows = pl.ds(c * ck, ck)
            p = jnp.exp2(s_scr[h, rows, :] - m_old)
            lc = jnp.sum(p, axis=0, keepdims=True)
            pv = jnp.dot(vt_ref[h, :, rows], p.astype(BF16), preferred_element_type=F32)
            st = lax.dot_general(k_ref[h, rows, :], q, (((1,), (1,)), ((), ())),
                                 preferred_element_type=F32)
            s_scr[h, rows, :] = st
            mc = jnp.max(st, axis=0, keepdims=True)
            l = lc if l is None else l + lc
            acc = pv if acc is None else acc + pv
            m_new = mc if m_new is None else jnp.maximum(m_new, mc)
        m_scr[h] = jnp.broadcast_to(m_new, m_scr.shape[1:])
        o_ref[:, h * V_HEAD_DIM:(h + 1) * V_HEAD_DIM] = (acc * (1.0 / l)).T.astype(BF16)


def _attn_pipe_call(p, qpe, k, vt, seq_len, tq):
    t = p.shape[0]
    sk = k.shape[2]
    q_tiles = seq_len // tq
    qn_col0 = P_OFF_QN // (2 * QK_NOPE_DIM)

    def q_row(b, qi):
        return b * q_tiles + jnp.minimum(qi, q_tiles - 1)

    def o_row(b, qi):
        return b * q_tiles + jnp.maximum(qi - 1, 0)

    return pl.pallas_call(
        functools.partial(_attn_pipe_kernel, ck=512),
        grid=(t // seq_len, N_HEADS // 2, q_tiles + 1),
        in_specs=[pl.BlockSpec((tq, 2 * QK_NOPE_DIM), lambda b, hp, qi: (q_row(b, qi), qn_col0 + hp)),
                  pl.BlockSpec((tq, 2 * QK_ROPE_DIM), lambda b, hp, qi: (q_row(b, qi), hp)),
                  pl.BlockSpec((None, 2, sk, KEY_WIDTH), lambda b, hp, qi: (b, hp, 0, 0)),
                  pl.BlockSpec((None, 2, V_HEAD_DIM, sk), lambda b, hp, qi: (b, hp, 0, 0))],
        out_specs=pl.BlockSpec((tq, 2 * V_HEAD_DIM), lambda b, hp, qi: (o_row(b, qi), hp)),
        out_shape=jax.ShapeDtypeStruct((t, N_HEADS * V_HEAD_DIM), BF16),
        scratch_shapes=[pltpu.VMEM((2, sk, tq), F32), pltpu.VMEM((2, 8, tq), F32)],
        compiler_params=_cparams(3, 48),
        name="attn_pipe",
    )(p, qpe, k, vt)


def _attn_call(p, qpe, k, vt, seq_len, tq):
    t = p.shape[0]
    sk = k.shape[2]
    q_tiles = seq_len // tq
    if q_tiles > 1:
        return _attn_pipe_call(p, qpe, k, vt, seq_len, tq)
    hps = 8
    assert P_OFF_QN % (hps * QK_NOPE_DIM) == 0
    qn_col0 = P_OFF_QN // (hps * QK_NOPE_DIM)
    return pl.pallas_call(
        _attn_kernel,
        grid=(t // seq_len, N_HEADS // hps, q_tiles),
        in_specs=[pl.BlockSpec((tq, hps * QK_NOPE_DIM), lambda b, hp, qi: (b * q_tiles + qi, qn_col0 + hp)),
                  pl.BlockSpec((tq, hps * QK_ROPE_DIM), lambda b, hp, qi: (b * q_tiles + qi, hp)),
                  pl.BlockSpec((None, hps, sk, KEY_WIDTH), lambda b, hp, qi: (b, hp, 0, 0)),
                  pl.BlockSpec((None, hps, V_HEAD_DIM, sk), lambda b, hp, qi: (b, hp, 0, 0))],
        out_specs=pl.BlockSpec((tq, hps * V_HEAD_DIM), lambda b, hp, qi: (b * q_tiles + qi, hp)),
        out_shape=jax.ShapeDtypeStruct((t, N_HEADS * V_HEAD_DIM), BF16),
        compiler_params=_cparams(3, 48),
        name="attn",
    )(p, qpe, k, vt)


def _column_chunk_epilogue(step, first_step, n_chunks, tn, out_ref, x_ref, gate_row, y):
    for n in range(n_chunks):
        @pl.when(step == first_step + n)
        def _(n=n):
            cols = slice(n * tn, (n + 1) * tn)
            out_ref[:, cols] = x_ref[:, cols] + gate_row[:, cols] * y


def _merge_kernel(za_ref, o_ref, uc_ref, g0_ref, g1_ref, g2_ref, x_ref, mod_ref,
                  woa_ref, wo_ref, woc_ref, wm_ref, out_ref, m_ref, *, n_mix, tn):
    s = pl.program_id(1)
    tk = m_ref.shape[2]

    @pl.when(s < n_mix)
    def _():
        def sig(r):
            return jax.nn.sigmoid(r[...].astype(F32))

        ya = jnp.dot(za_ref[...], woa_ref[...], preferred_element_type=F32)
        yb = jnp.dot(o_ref[...], wo_ref[...], preferred_element_type=F32)
        yc = jnp.dot(uc_ref[...], woc_ref[...], preferred_element_type=F32)
        merged = sig(g0_ref) * ya + sig(g1_ref) * yb + sig(g2_ref) * yc
        m_ref[s] = merged.astype(BF16)

    @pl.when(s >= n_mix)
    def _():
        y = None
        for k in range(n_mix):
            part = jnp.dot(m_ref[k], wm_ref[k * tk:(k + 1) * tk, :], preferred_element_type=F32)
            y = part if y is None else y + part
        _column_chunk_epilogue(s, n_mix, D_MODEL // tn, tn, out_ref, x_ref, mod_ref[2:3, :], y)


def _merge_call(za, o, uc, p, x, mod_l, w_out_a, w_o, w_out_c, w_merge, row_fn, tm):
    t = x.shape[0]
    tn = 512
    n_mix = D_MODEL // tn
    gate_col0 = P_OFF_GATE // tn

    def mix(s):
        return jnp.minimum(s, n_mix - 1)

    def out_col(s):
        return jnp.maximum(s - n_mix, 0)

    def gate_spec(g):
        return pl.BlockSpec((tm, tn), lambda i, s: (i, gate_col0 + g * n_mix + mix(s)))

    return pl.pallas_call(
        functools.partial(_merge_kernel, n_mix=n_mix, tn=tn),
        grid=(t // tm, n_mix + D_MODEL // tn),
        in_specs=[pl.BlockSpec((tm, D_CONV_A), lambda i, s: (i, 0)),
                  pl.BlockSpec((tm, D_MODEL), lambda i, s: (i, 0)),
                  pl.BlockSpec((tm, D_CONF), lambda i, s: (i, 0)),
                  gate_spec(0), gate_spec(1), gate_spec(2),
                  pl.BlockSpec((tm, D_MODEL), lambda i, s: (i, 0)),
                  pl.BlockSpec((None, 6, D_MODEL), lambda i, s: (row_fn(i * tm), 0, 0)),
                  pl.BlockSpec((D_CONV_A, tn), lambda i, s: (0, mix(s))),
                  pl.BlockSpec((D_MODEL, tn), lambda i, s: (0, mix(s))),
                  pl.BlockSpec((D_CONF, tn), lambda i, s: (0, mix(s))),
                  pl.BlockSpec((D_MODEL, tn), lambda i, s: (0, out_col(s)))],
        out_specs=pl.BlockSpec((tm, D_MODEL), lambda i, s: (i, 0)),
        out_shape=jax.ShapeDtypeStruct((t, D_MODEL), F32),
        scratch_shapes=[pltpu.VMEM((n_mix, tm, tn), BF16)],
        compiler_params=_cparams(2, 52),
        name="merge",
    )(za, o, uc, p, p, p, x, mod_l, w_out_a, w_o, w_out_c, w_merge)


def _ffn_kernel(*refs, final_norm, n_up, tn):
    if final_norm:
        x_ref, mod_ref, g_ref, w1_ref, w2_ref, fg_ref, out_ref, h_ref, a_ref = refs
    else:
        x_ref, mod_ref, g_ref, w1_ref, w2_ref, out_ref, h_ref, a_ref = refs
    s = pl.program_id(1)
    tf = a_ref.shape[2]
    n_down = D_MODEL // tn
    n_row_chunks = x_ref.shape[0] // ROW_CHUNK

    @pl.when(s == 0)
    def _():
        g = g_ref[...]
        shift = mod_ref[3:4, :]
        scale = mod_ref[4:5, :]

        def body(r, carry):
            rows = pl.ds(pl.multiple_of(r * ROW_CHUNK, ROW_CHUNK), ROW_CHUNK)
            h_ref[rows, :] = _modulated_rmsnorm_rows(x_ref[rows, :], g, scale, shift).astype(BF16)
            return carry

        lax.fori_loop(0, n_row_chunks, body, 0, unroll=ROW_UNROLL)

    @pl.when(s < n_up)
    def _():
        a = jnp.dot(h_ref[...], w1_ref[...], preferred_element_type=F32)
        a_ref[s] = jnp.square(jnp.maximum(a, 0.0)).astype(BF16)

    @pl.when(s >= n_up)
    def _():
        y = None
        for k in range(n_up):
            part = jnp.dot(a_ref[k], w2_ref[k * tf:(k + 1) * tf, :], preferred_element_type=F32)
            y = part if y is None else y + part
        _column_chunk_epilogue(s, n_up, n_down, tn, out_ref, x_ref, mod_ref[5:6, :], y)

    if final_norm:
        @pl.when(s == n_up + n_down - 1)
        def _():
            fg = fg_ref[...]

            def body(r, carry):
                rows = pl.ds(pl.multiple_of(r * ROW_CHUNK, ROW_CHUNK), ROW_CHUNK)
                y = out_ref[rows, :]
                ms = jnp.mean(y * y, axis=-1, keepdims=True)
                out_ref[rows, :] = y * lax.rsqrt(ms + NORM_EPS) * fg
                return carry

            lax.fori_loop(0, n_row_chunks, body, 0, unroll=ROW_UNROLL)


def _ffn_call(x, mod_l, g2, w1, w2, final_g, row_fn, tm):
    t = x.shape[0]
    tf = 1024
    tn = 512
    n_up = D_FF // tf
    final_norm = final_g is not None

    def up(s):
        return jnp.minimum(s, n_up - 1)

    def down(s):
        return jnp.maximum(s - n_up, 0)

    in_specs = [pl.BlockSpec((tm, D_MODEL), lambda i, s: (i, 0)),
                pl.BlockSpec((None, 6, D_MODEL), lambda i, s: (row_fn(i * tm), 0, 0)),
                pl.BlockSpec((1, D_MODEL), lambda i, s: (0, 0)),
                pl.BlockSpec((D_MODEL, tf), lambda i, s: (0, up(s))),
                pl.BlockSpec((D_FF, tn), lambda i, s: (0, down(s)))]
    args = [x, mod_l, g2, w1, w2]
    if final_norm:
        in_specs.append(pl.BlockSpec((1, D_MODEL), lambda i, s: (0, 0)))
        args.append(final_g)
    return pl.pallas_call(
        functools.partial(_ffn_kernel, final_norm=final_norm, n_up=n_up, tn=tn),
        grid=(t // tm, n_up + D_MODEL // tn),
        in_specs=in_specs,
        out_specs=pl.BlockSpec((tm, D_MODEL), lambda i, s: (i, 0)),
        out_shape=jax.ShapeDtypeStruct((t, D_MODEL), F32),
        scratch_shapes=[pltpu.VMEM((tm, D_MODEL), BF16), pltpu.VMEM((n_up, tm, tf), BF16)],
        compiler_params=_cparams(2, 60),
        name="ffn",
    )(*args)


def _cast_kernel(w_ref, o_ref):
    o_ref[...] = w_ref[...].astype(BF16)


def _cast_layer_call(w_stacked, l):
    _, rows, cols = w_stacked.shape
    tr = max(BF16_ROWS, min(rows, (4 * MIB) // (4 * cols)))
    assert rows % tr == 0
    return pl.pallas_call(
        _cast_kernel,
        grid=(rows // tr,),
        in_specs=[pl.BlockSpec((None, tr, cols), lambda i: (l, i, 0))],
        out_specs=pl.BlockSpec((tr, cols), lambda i: (i, 0)),
        out_shape=jax.ShapeDtypeStruct((rows, cols), BF16),
        compiler_params=_cparams(1, 32),
        name="cast",
    )(w_stacked)


def _repack_w_in_kernel(w_ref, main_ref, small_ref):
    w = w_ref[...]
    main_ref[:, P_OFF_A:P_OFF_QN] = w[:, :OFF_Q].astype(BF16)
    for h in range(N_HEADS):
        base = OFF_Q + h * QK_HEAD_DIM
        main_ref[:, P_OFF_QN + h * QK_NOPE_DIM:P_OFF_QN + (h + 1) * QK_NOPE_DIM] = (
            w[:, base:base + QK_NOPE_DIM].astype(BF16))
        small_ref[:, h * QK_ROPE_DIM:(h + 1) * QK_ROPE_DIM] = (
            w[:, base + QK_NOPE_DIM:base + QK_HEAD_DIM].astype(BF16))
    main_ref[:, P_OFF_CONF:] = w[:, OFF_CONF:].astype(BF16)
    small_ref[:, S_OFF_CKV:S_OFF_CKV + SPLIT_KV] = w[:, OFF_KV:OFF_CONF].astype(BF16)
    small_ref[:, S_OFF_CKV + SPLIT_KV:] = jnp.zeros((w.shape[0], S_WIDTH - S_OFF_CKV - SPLIT_KV), BF16)


def _repack_w_in_call(w_in, l):
    _, rows, cols = w_in.shape
    tr = 64
    return pl.pallas_call(
        _repack_w_in_kernel,
        grid=(rows // tr,),
        in_specs=[pl.BlockSpec((None, tr, cols), lambda i: (l, i, 0))],
        out_specs=[pl.BlockSpec((tr, P_WIDTH), lambda i: (i, 0)),
                   pl.BlockSpec((tr, S_WIDTH), lambda i: (i, 0))],
        out_shape=[jax.ShapeDtypeStruct((rows, P_WIDTH), BF16),
                   jax.ShapeDtypeStruct((rows, S_WIDTH), BF16)],
        compiler_params=_cparams(1, 32),
        name="repack",
    )(w_in)


def _rope_tables(n_tokens):
    pos = jnp.arange(n_tokens)
    row = (pos // GRID_W).astype(F32)
    col = (pos % GRID_W).astype(F32)
    inv_freq = ROPE_BASE ** (-jnp.arange(N_FREQ, dtype=F32) / N_FREQ)
    ang = jnp.stack([row[:, None] * inv_freq, col[:, None] * inv_freq], axis=1)
    cos, sin = jnp.cos(ang), jnp.sin(ang)
    zero = jnp.zeros_like(sin)
    cos64 = jnp.stack([cos, cos], axis=2).reshape(n_tokens, QK_ROPE_DIM)
    up64 = jnp.stack([-sin, zero], axis=2).reshape(n_tokens, QK_ROPE_DIM)
    dn64 = jnp.stack([zero, sin], axis=2).reshape(n_tokens, QK_ROPE_DIM)
    return tuple(jnp.concatenate([a, a], axis=1) for a in (cos64, up64, dn64))


def _prep_layer_weights(l, w_in, w_kv_b, w_o_attn, conf_dw_w):
    w_main, w_small = _repack_w_in_call(w_in, l)
    w_k =w_kv_b[l][:, :, :QK_NOPE_DIM].reshape(KV_LORA_RANK, N_HEADS * QK_NOPE_DIM).astype(BF16)
    w_vt = jnp.transpose(w_kv_b[l][:, :, QK_NOPE_DIM:], (1, 2, 0)).reshape(
        N_HEADS * V_HEAD_DIM, KV_LORA_RANK).astype(BF16)
    w_o = _cast_layer_call(w_o_attn.reshape(DEPTH, N_HEADS * V_HEAD_DIM, D_MODEL), l)
    wc = jnp.pad(conf_dw_w[l], ((0, 1), (0, 0)))
    wc = wc.reshape(CONF_WIDTH + 1, D_CONF // LANES, LANES).transpose(1, 0, 2)
    return w_main, w_small, w_k, w_vt, w_o, wc


def _run_group(x, seq_len, mods, row_fn, rope_tabs, caches, lw, final_g, tm, ts, tq):
    t = x.shape[0]
    b = t // seq_len
    ckv_layers, kpe_layers = [], []
    for l in range(DEPTH):
        w = lw[l]
        mod_l = mods[l]
        p, h = _inproj_call(x, mod_l, w["g1"], w["w_main"], row_fn, tm)
        outs = _qkvs_call(h, w["w_small"], w["kv_g"], rope_tabs, 512)
        qpe, ckv, kpe = outs[:3]
        ckv_layers.append(ckv)
        kpe_layers.append(kpe)
        za, uc = _mixers_call(p, w["conv_a_w"], w["wc"], w["conf_b"], w["ln_g"], w["ln_b"], seq_len, ts)
        ckv_all = ckv.reshape(b, seq_len, KV_LORA_RANK)
        if rope_tabs is None:
            kpe_all = kpe.reshape(b, seq_len, QK_ROPE_DIM)
        else:
            cache_ckv, cache_kpe = caches
            ckv_all = jnp.concatenate([ckv_all, cache_ckv[:, l]], axis=1)
            kpe_all = jnp.concatenate([outs[3].reshape(b, seq_len, QK_ROPE_DIM), cache_kpe[:, l]], axis=1)
        sk = ckv_all.shape[1]
        k, vt = _kvup_call(ckv_all, kpe_all, w["w_k"], w["w_vt"], 512 if sk % 512 == 0 else sk)
        o = _attn_call(p, qpe, k, vt, seq_len, tq)
        x = _merge_call(za, o, uc, p, x, mod_l, w["w_out_a"], w["w_o"], w["w_out_c"], w["w_merge"], row_fn, 512)
        x = _ffn_call(x, mod_l, w["g2"], w["w_ff1"], w["w_ff2"],
                      final_g if l == DEPTH - 1 else None, row_fn, 512)
    return x, ckv_layers, kpe_layers


def kernel(x_prompt, x_sample, cache_ckv, cache_kpe, c, c_ctx, w_mod, b_mod, norm1_g, w_in, conv_a_w, w_out_a, kv_norm_g, w_kv_b, w_o_attn, conf_dw_w, conf_dw_b, conf_ln_g, conf_ln_b, w_out_c, w_merge, norm2_g, w_ff1, w_ff2, final_norm_g):
    batch, seq, _ = x_prompt.shape
    dec_batch, dec_seq, _ = x_sample.shape

    n_rows = 8
    c_rows = jnp.concatenate(
        [c_ctx[None, :], c, jnp.zeros((n_rows - 1 - dec_batch, D_MODEL), F32)], axis=0)
    mod = _mod_call(c_rows, w_mod, b_mod).reshape(DEPTH, n_rows, 6, D_MODEL)
    mods = [mod[l] for l in range(DEPTH)]

    lw = []
    for l in range(DEPTH):
        w_main, w_small, w_k, w_vt, w_o, wc = _prep_layer_weights(l, w_in, w_kv_b, w_o_attn, conf_dw_w)
        lw.append(dict(
            g1=norm1_g[l][None, :], g2=norm2_g[l][None, :], kv_g=kv_norm_g[l][None, :],
            w_main=w_main, w_small=w_small, w_k=w_k, w_vt=w_vt, w_o=w_o, wc=wc,
            conv_a_w=conv_a_w[l], conf_b=conf_dw_b[l][None, :],
            ln_g=conf_ln_g[l][None, :], ln_b=conf_ln_b[l][None, :],
            w_out_a=_cast_layer_call(w_out_a, l), w_out_c=_cast_layer_call(w_out_c, l),
            w_merge=_cast_layer_call(w_merge, l),
            w_ff1=_cast_layer_call(w_ff1, l), w_ff2=_cast_layer_call(w_ff2, l)))
    final_g = final_norm_g[None, :]

    tm = 1024
    yp, ckv_layers, kpe_layers = _run_group(
        x_prompt.reshape(batch * seq, D_MODEL), seq, mods, lambda tok: 0, None, None,
        lw, final_g, tm, seq, seq)
    ys, _, _ = _run_group(
        x_sample.reshape(dec_batch * dec_seq, D_MODEL), dec_seq, mods,
        lambda tok: 1 + tok // dec_seq, _rope_tables(dec_seq), (cache_ckv, cache_kpe),
        lw, final_g, tm, 256, 256)

    new_ckv = jnp.stack([a.reshape(batch, seq, KV_LORA_RANK) for a in ckv_layers], axis=1)
    new_kpe = jnp.stack([a.reshape(batch, seq, QK_ROPE_DIM) for a in kpe_layers], axis=1)
    return (yp.reshape(batch, seq, D_MODEL), ys.reshape(dec_batch, dec_seq, D_MODEL), new_ckv, new_kpe)
```

```python
import functools

import jax
import jax.numpy as jnp
from jax import lax
from jax.experimental import pallas as pl
from jax.experimental.pallas import tpu as pltpu

F32 = jnp.float32
BF16 = jnp.bfloat16

D_MODEL = 2048
DEPTH = 2
GRID_W = 64
D_CONV_A = 1024
CONV_A_WIDTH = 3
N_HEADS = 16
QK_NOPE_DIM = 128
QK_ROPE_DIM = 64
V_HEAD_DIM = 128
KV_LORA_RANK = 512
QK_HEAD_DIM = QK_NOPE_DIM + QK_ROPE_DIM
N_FREQ = QK_ROPE_DIM // 4
ROPE_BASE = 10000.0
ATTN_SCALE = QK_HEAD_DIM ** -0.5
SCORE_SCALE = ATTN_SCALE * 1.4426950408889634
D_CONF = 1024
CONF_WIDTH = 31
D_FF = 4 * D_MODEL
NORM_EPS = 1e-6

SPLIT_A = 3 * D_CONV_A
SPLIT_Q = N_HEADS * QK_HEAD_DIM
SPLIT_KV = KV_LORA_RANK + QK_ROPE_DIM
SPLIT_CONF = 2 * D_CONF
OFF_Q = SPLIT_A
OFF_KV = OFF_Q + SPLIT_Q
OFF_CONF = OFF_KV + SPLIT_KV
OFF_GATE = OFF_CONF + SPLIT_CONF

LANES = 128
BF16_ROWS = 16
MIB = 1024 * 1024

P_OFF_A = 0
P_OFF_QN = SPLIT_A
P_OFF_CONF = P_OFF_QN + N_HEADS * QK_NOPE_DIM
P_OFF_GATE = P_OFF_CONF + SPLIT_CONF
P_WIDTH = P_OFF_GATE + 3 * D_MODEL
S_OFF_CKV = N_HEADS * QK_ROPE_DIM
S_OFF_KPE = S_OFF_CKV + KV_LORA_RANK
S_WIDTH = S_OFF_KPE + LANES
KEY_WIDTH = 2 * LANES

ROW_CHUNK = 16
ROW_UNROLL = 4
CONV_HALO = 16


def _cparams(n_axes, vmem_mib):
    return pltpu.CompilerParams(dimension_semantics=("arbitrary",) * n_axes,
                                vmem_limit_bytes=vmem_mib * MIB)


def _mod_kernel(c_ref, w_ref, b_ref, o_ref):
    c = c_ref[...]
    s = c * jax.nn.sigmoid(c)
    o_ref[...] = jnp.dot(s.astype(BF16), w_ref[...].astype(BF16),
                         preferred_element_type=F32) + b_ref[...]


def _mod_call(c_rows, w_mod, b_mod):
    n_rows = c_rows.shape[0]
    n = 6 * D_MODEL
    tn = 1024
    return pl.pallas_call(
        _mod_kernel,
        grid=(DEPTH, n // tn),
        in_specs=[pl.BlockSpec((n_rows, D_MODEL), lambda l, j: (0, 0)),
                  pl.BlockSpec((None, D_MODEL, tn), lambda l, j: (l, 0, j)),
                  pl.BlockSpec((None, 1, tn), lambda l, j: (l, 0, j))],
        out_specs=pl.BlockSpec((None, n_rows, tn), lambda l, j: (l, 0, j)),
        out_shape=jax.ShapeDtypeStruct((DEPTH, n_rows, n), F32),
        compiler_params=_cparams(2, 40),
        name="mod",
    )(c_rows, w_mod, b_mod.reshape(DEPTH, 1, n))


def _modulated_rmsnorm_rows(x, g, scale, shift):
    ms = jnp.mean(x * x, axis=-1, keepdims=True)
    y = x * lax.rsqrt(ms + NORM_EPS) * g
    return y * (1.0 + scale) + shift


def _dot_nt(a, b_t):
    return lax.dot_general(a, b_t, (((1,), (1,)), ((), ())), preferred_element_type=F32)


def _rope_lanes(x, cos, s_up, s_dn):
    up = pltpu.roll(x, LANES - N_FREQ, 1)
    dn = pltpu.roll(x, N_FREQ, 1)
    return x * cos + up * s_up + dn * s_dn


def _inproj_kernel(x_ref, mod_ref, g_ref, w_ref, p_ref, h_ref):
    @pl.when(pl.program_id(1) == 0)
    def _():
        g = g_ref[...]
        shift = mod_ref[0:1, :]
        scale = mod_ref[1:2, :]

        def body(r, carry):
            rows = pl.ds(pl.multiple_of(r * ROW_CHUNK, ROW_CHUNK), ROW_CHUNK)
            h = _modulated_rmsnorm_rows(x_ref[rows, :], g, scale, shift)
            h_ref[rows, :] = h.astype(BF16)
            return carry

        lax.fori_loop(0, x_ref.shape[0] // ROW_CHUNK, body, 0, unroll=ROW_UNROLL)

    p_ref[...] = _dot_nt(h_ref[...], w_ref[...]).astype(BF16)


def _inproj_call(x, mod_l, g1, w_main_t, row_fn, tm):
    t = x.shape[0]
    tn = 1024
    return pl.pallas_call(
        _inproj_kernel,
        grid=(t // tm, P_WIDTH // tn),
        in_specs=[pl.BlockSpec((tm, D_MODEL), lambda i, j: (i, 0)),
                  pl.BlockSpec((None, 6, D_MODEL), lambda i, j: (row_fn(i * tm), 0, 0)),
                  pl.BlockSpec((1, D_MODEL), lambda i, j: (0, 0)),
                  pl.BlockSpec((tn, D_MODEL), lambda i, j: (j, 0))],
        out_specs=[pl.BlockSpec((tm, tn), lambda i, j: (i, j)),
                   pl.BlockSpec((tm, D_MODEL), lambda i, j: (i, 0))],
        out_shape=[jax.ShapeDtypeStruct((t, P_WIDTH), BF16),
                   jax.ShapeDtypeStruct((t, D_MODEL), BF16)],
        compiler_params=_cparams(2, 48),
        name="inproj",
    )(x, mod_l, g1, w_main_t)


def _qkvs_kernel(*refs, rope):
    if rope:
        h_ref, w_ref, kvg_ref, cos_ref, sup_ref, sdn_ref, qpe_ref, ckv_ref, kpe_ref, kpr_ref = refs
    else:
        h_ref, w_ref, kvg_ref, qpe_ref, ckv_ref, kpe_ref = refs
    acc = _dot_nt(h_ref[...], w_ref[...])
    if rope:
        cos, s_up, s_dn = cos_ref[...], sup_ref[...], sdn_ref[...]
    for c in range(S_OFF_CKV // LANES):
        q = acc[:, c * LANES:(c + 1) * LANES]
        if rope:
            q = _rope_lanes(q, cos, s_up, s_dn)
        qpe_ref[:, c * LANES:(c + 1) * LANES] = q.astype(BF16)
    ckv = acc[:, S_OFF_CKV:S_OFF_KPE]
    ms = jnp.mean(ckv * ckv, axis=-1, keepdims=True)
    ckv_ref[...] = ckv * lax.rsqrt(ms + NORM_EPS) * kvg_ref[...]
    kpe2 = acc[:, S_OFF_KPE:S_WIDTH]
    kpe_ref[...] = kpe2[:, :QK_ROPE_DIM]
    if rope:
        kpr_ref[...] = _rope_lanes(kpe2, cos, s_up, s_dn)[:, :QK_ROPE_DIM]


def _qkvs_call(h, w_small_t, kv_g, rope_tabs, tm):
    t = h.shape[0]
    rope = rope_tabs is not None
    in_specs = [pl.BlockSpec((tm, D_MODEL), lambda i: (i, 0)),
                pl.BlockSpec((S_WIDTH, D_MODEL), lambda i: (0, 0)),
                pl.BlockSpec((1, KV_LORA_RANK), lambda i: (0, 0))]
    args = [h, w_small_t, kv_g]
    out_specs = [pl.BlockSpec((tm, S_OFF_CKV), lambda i: (i, 0)),
                 pl.BlockSpec((tm, KV_LORA_RANK), lambda i: (i, 0)),
                 pl.BlockSpec((tm, QK_ROPE_DIM), lambda i: (i, 0))]
    out_shape = [jax.ShapeDtypeStruct((t, S_OFF_CKV), BF16),
                 jax.ShapeDtypeStruct((t, KV_LORA_RANK), F32),
                 jax.ShapeDtypeStruct((t, QK_ROPE_DIM), F32)]
    if rope:
        seq_tiles = rope_tabs[0].shape[0] // tm
        in_specs += [pl.BlockSpec((tm, LANES), lambda i: (i % seq_tiles, 0))] * 3
        args += list(rope_tabs)
        out_specs.append(pl.BlockSpec((tm, QK_ROPE_DIM), lambda i: (i, 0)))
        out_shape.append(jax.ShapeDtypeStruct((t, QK_ROPE_DIM), F32))
    return pl.pallas_call(
        functools.partial(_qkvs_kernel, rope=rope),
        grid=(t // tm,),
        in_specs=in_specs, out_specs=out_specs, out_shape=out_shape,
        compiler_params=_cparams(1, 40),
        name="qkvs",
    )(*args)


def _mixers_kernel(xin, gb, gc, ca, cb, xin_p, gc_p, ca_p, cb_p, xin_n, gc_n, ca_n, cb_n,
                   wa_ref, wc_ref, cbias_ref, lng_ref, lnb_ref, za_ref, uc_ref,
                   bufa, bufc, convc, *, ts, tiles_per_seq):
    i = pl.program_id(0)
    first = (i % tiles_per_seq) == 0
    last = (i % tiles_per_seq) == tiles_per_seq - 1
    hl = CONV_HALO

    def f32(r):
        return r[...].astype(F32)

    bufa[0:hl, :] = jnp.where(first, 0.0, f32(gc_p) * f32(xin_p))
    bufa[hl:hl + ts, :] = f32(gc) * f32(xin)
    bufa[hl + ts:, :] = jnp.where(last, 0.0, f32(gc_n) * f32(xin_n))
    pad_a = (CONV_A_WIDTH - 1) // 2
    y = jnp.zeros((ts, D_CONV_A), F32)
    for k in range(CONV_A_WIDTH):
        y = y + wa_ref[k:k + 1, :] * bufa[pl.ds(hl - pad_a + k, ts), :]
    za_ref[...] = (f32(gb) * y).astype(BF16)

    def glu(a, b):
        return f32(a) * jax.nn.sigmoid(f32(b))

    n_chunks = D_CONF // LANES
    up = jnp.where(first, 0.0, glu(ca_p, cb_p))
    um = glu(ca, cb)
    un = jnp.where(last, 0.0, glu(ca_n, cb_n))
    for c in range(n_chunks):
        lanes = slice(c * LANES, (c + 1) * LANES)
        bufc[c, 0:hl, :] = up[:, lanes]
        bufc[c, hl:hl + ts, :] = um[:, lanes]
        bufc[c, hl + ts:, :] = un[:, lanes]
    pad_c = (CONF_WIDTH - 1) // 2

    def chunk_body(c, carry):
        w = wc_ref[c]
        acc = jnp.zeros((ts, LANES), F32)
        for k in range(CONF_WIDTH):
            acc = acc + w[k:k + 1, :] * bufc[c, pl.ds(hl - pad_c + k, ts), :]
        convc[c] = acc
        return carry

    lax.fori_loop(0, n_chunks, chunk_body, 0)
    u = jnp.concatenate([convc[c] for c in range(n_chunks)], axis=1) + cbias_ref[...]
    mu = jnp.mean(u, axis=-1, keepdims=True)
    uc = u - mu
    var = jnp.mean(uc * uc, axis=-1, keepdims=True)
    v = uc * lax.rsqrt(var + NORM_EPS) * lng_ref[...] + lnb_ref[...]
    uc_ref[...] = (v * jax.nn.sigmoid(v)).astype(BF16)


def _mixers_call(p, conv_a_w, conf_w_chunks, conf_b, ln_g, ln_b, seq_len, ts):
    t = p.shape[0]
    tiles_per_seq = seq_len // ts
    halo_per_tile = ts // CONV_HALO
    n_halo_blocks = t // CONV_HALO
    col_a = P_OFF_A // D_CONV_A
    col_c = P_OFF_CONF // D_CONF
    cols = [col_a, col_a + 1, col_a + 2, col_c, col_c + 1]
    halo_cols = [col_a, col_a + 2, col_c, col_c + 1]

    def main_spec(col):
        return pl.BlockSpec((ts, D_CONF), lambda i: (i, col))

    def prev_spec(col):
        return pl.BlockSpec((CONV_HALO, D_CONF),
                            lambda i: (jnp.maximum(i * halo_per_tile - 1, 0), col))

    def next_spec(col):
        return pl.BlockSpec((CONV_HALO, D_CONF),
                            lambda i: (jnp.minimum((i + 1) * halo_per_tile, n_halo_blocks - 1), col))

    def full(a):
        return pl.BlockSpec(a.shape, lambda i: (0,) * a.ndim)

    small = [conv_a_w, conf_w_chunks, conf_b, ln_g, ln_b]
    in_specs = ([main_spec(c) for c in cols] + [prev_spec(c) for c in halo_cols]
                + [next_spec(c) for c in halo_cols] + [full(a) for a in small])
    return pl.pallas_call(
        functools.partial(_mixers_kernel, ts=ts, tiles_per_seq=tiles_per_seq),
        grid=(t // ts,),
        in_specs=in_specs,
        out_specs=[pl.BlockSpec((ts, D_CONV_A), lambda i: (i, 0)),
                   pl.BlockSpec((ts, D_CONF), lambda i: (i, 0))],
        out_shape=[jax.ShapeDtypeStruct((t, D_CONV_A), BF16),
                   jax.ShapeDtypeStruct((t, D_CONF), BF16)],
        scratch_shapes=[pltpu.VMEM((ts + 2 * CONV_HALO, D_CONV_A), F32),
                        pltpu.VMEM((D_CONF // LANES, ts + 2 * CONV_HALO, LANES), F32),
                        pltpu.VMEM((D_CONF // LANES, ts, LANES), F32)],
        compiler_params=_cparams(1, 40),
        name="mixers",
    )(*([p] * 13), *small)


def _kvup_kernel(ckv_ref, kpe_ref, wk_ref, wvt_ref, k_ref, vt_ref):
    ckv = ckv_ref[...].astype(BF16)
    kpe = (kpe_ref[...] * SCORE_SCALE).astype(BF16)
    zeros = jnp.zeros_like(kpe)
    kn = (jnp.dot(ckv, wk_ref[...], preferred_element_type=F32) * SCORE_SCALE).astype(BF16)
    vt = lax.dot_general(wvt_ref[...], ckv, (((1,), (1,)), ((), ())),
                         preferred_element_type=F32).astype(BF16)
    for h in range(N_HEADS):
        k_ref[h, :, 0:QK_NOPE_DIM] = kn[:, h * QK_NOPE_DIM:(h + 1) * QK_NOPE_DIM]
        lo, hi = (kpe, zeros) if h % 2 == 0 else (zeros, kpe)
        k_ref[h, :, QK_NOPE_DIM:QK_NOPE_DIM + QK_ROPE_DIM] = lo
        k_ref[h, :, QK_NOPE_DIM + QK_ROPE_DIM:] = hi
        vt_ref[h] = vt[h * V_HEAD_DIM:(h + 1) * V_HEAD_DIM, :]


def _kvup_call(ckv_all, kpe_all, w_k, w_vt, tk):
    b, sk, _ = ckv_all.shape
    return pl.pallas_call(
        _kvup_kernel,
        grid=(b, sk // tk),
        in_specs=[pl.BlockSpec((None, tk, KV_LORA_RANK), lambda bi, s: (bi, s, 0)),
                  pl.BlockSpec((None, tk, QK_ROPE_DIM), lambda bi, s: (bi, s, 0)),
                  pl.BlockSpec(w_k.shape, lambda bi, s: (0, 0)),
                  pl.BlockSpec(w_vt.shape, lambda bi, s: (0, 0))],
        out_specs=[pl.BlockSpec((None, N_HEADS, tk, KEY_WIDTH), lambda bi, s: (bi, 0, s, 0)),
                   pl.BlockSpec((None, N_HEADS, V_HEAD_DIM, tk), lambda bi, s: (bi, 0, 0, s))],
        out_shape=[jax.ShapeDtypeStruct((b, N_HEADS, sk, KEY_WIDTH), BF16),
                   jax.ShapeDtypeStruct((b, N_HEADS, V_HEAD_DIM, sk), BF16)],
        compiler_params=_cparams(2, 40),
        name="kvup",
    )(ckv_all, kpe_all, w_k, w_vt)


def _attn_kernel(qn_ref, qp_ref, k_ref, vt_ref, o_ref):
    for h in range(k_ref.shape[0]):
        pair = (h // 2) * LANES
        q = jnp.concatenate([qn_ref[:, h * QK_NOPE_DIM:(h + 1) * QK_NOPE_DIM],
                             qp_ref[:, pair:pair + LANES]], axis=1)
        st = lax.dot_general(k_ref[h], q, (((1,), (1,)), ((), ())), preferred_element_type=F32)
        m = jnp.max(st, axis=0, keepdims=True)
        p = jnp.exp2(st - m)
        l = jnp.sum(p, axis=0, keepdims=True)
        ot = jnp.dot(vt_ref[h], p.astype(BF16), preferred_element_type=F32)
        o_ref[:, h * V_HEAD_DIM:(h + 1) * V_HEAD_DIM] = (ot * (1.0 / l)).T.astype(BF16)


def _attn_pipe_kernel(qn_ref, qp_ref, k_ref, vt_ref, o_ref, s_scr, m_scr, *, ck):
    sk = k_ref.shape[1]

    @pl.when(pl.program_id(2) == 0)
    def _():
        s_scr[...] = jnp.zeros_like(s_scr)
        m_scr[...] = jnp.zeros_like(m_scr)

    for h in range(k_ref.shape[0]):
        pair = (h // 2) * LANES
        q = jnp.concatenate([qn_ref[:, h * QK_NOPE_DIM:(h + 1) * QK_NOPE_DIM],
                             qp_ref[:, pair:pair + LANES]], axis=1)
        m_old = m_scr[h, 0:1, :]
        m_new = None
        l = None
        acc = None
        for c in range(sk // ck):
            rows = pl.ds(c * ck, ck)
            p = jnp.exp2(s_scr[h, rows, :] - m_old)
            lc = jnp.sum(p, axis=0, keepdims=True)
            pv = jnp.dot(vt_ref[h, :, rows], p.astype(BF16), preferred_element_type=F32)
            st = lax.dot_general(k_ref[h, rows, :], q, (((1,), (1,)), ((), ())),
                                 preferred_element_type=F32)
            s_scr[h, rows, :] = st
            mc = jnp.max(st, axis=0, keepdims=True)
            l = lc if l is None else l + lc
            acc = pv if acc is None else acc + pv
            m_new = mc if m_new is None else jnp.maximum(m_new, mc)
        m_scr[h] = jnp.broadcast_to(m_new, m_scr.shape[1:])
        o_ref[:, h * V_HEAD_DIM:(h + 1) * V_HEAD_DIM] = (acc * (1.0 / l)).T.astype(BF16)


def _attn_pipe_call(p, qpe, k, vt, seq_len, tq):
    t = p.shape[0]
    sk = k.shape[2]
    q_tiles = seq_len // tq
    hps = 4
    assert P_OFF_QN % (hps * QK_NOPE_DIM) == 0
    qn_col0 = P_OFF_QN // (hps * QK_NOPE_DIM)

    def q_row(b, qi):
        return b * q_tiles + jnp.minimum(qi, q_tiles - 1)

    def o_row(b, qi):
        return b * q_tiles + jnp.maximum(qi - 1, 0)

    return pl.pallas_call(
        functools.partial(_attn_pipe_kernel, ck=512),
        grid=(t // seq_len, N_HEADS // hps, q_tiles + 1),
        in_specs=[pl.BlockSpec((tq, hps * QK_NOPE_DIM), lambda b, hp, qi: (q_row(b, qi), qn_col0 + hp)),
                  pl.BlockSpec((tq, hps * QK_ROPE_DIM), lambda b, hp, qi: (q_row(b, qi), hp)),
                  pl.BlockSpec((None, hps, sk, KEY_WIDTH), lambda b, hp, qi: (b, hp, 0, 0)),
                  pl.BlockSpec((None, hps, V_HEAD_DIM, sk), lambda b, hp, qi: (b, hp, 0, 0))],
        out_specs=pl.BlockSpec((tq, hps * V_HEAD_DIM), lambda b, hp, qi: (o_row(b, qi), hp)),
        out_shape=jax.ShapeDtypeStruct((t, N_HEADS * V_HEAD_DIM), BF16),
        scratch_shapes=[pltpu.VMEM((hps, sk, tq), F32), pltpu.VMEM((hps, 8, tq), F32)],
        compiler_params=_cparams(3, 60),
        name="attn_pipe",
    )(p, qpe, k, vt)


def _attn_call(p, qpe, k, vt, seq_len, tq):
    t = p.shape[0]
    sk = k.shape[2]
    q_tiles = seq_len // tq
    if q_tiles > 1:
        return _attn_pipe_call(p, qpe, k, vt, seq_len, tq)
    hps = 8
    assert P_OFF_QN % (hps * QK_NOPE_DIM) == 0
    qn_col0 = P_OFF_QN // (hps * QK_NOPE_DIM)
    return pl.pallas_call(
        _attn_kernel,
        grid=(t // seq_len, N_HEADS // hps, q_tiles),
        in_specs=[pl.BlockSpec((tq, hps * QK_NOPE_DIM), lambda b, hp, qi: (b * q_tiles + qi, qn_col0 + hp)),
                  pl.BlockSpec((tq, hps * QK_ROPE_DIM), lambda b, hp, qi: (b * q_tiles + qi, hp)),
                  pl.BlockSpec((None, hps, sk, KEY_WIDTH), lambda b, hp, qi: (b, hp, 0, 0)),
                  pl.BlockSpec((None, hps, V_HEAD_DIM, sk), lambda b, hp, qi: (b, hp, 0, 0))],
        out_specs=pl.BlockSpec((tq, hps * V_HEAD_DIM), lambda b, hp, qi: (b * q_tiles + qi, hp)),
        out_shape=jax.ShapeDtypeStruct((t, N_HEADS * V_HEAD_DIM), BF16),
        compiler_params=_cparams(3, 48),
        name="attn",
    )(p, qpe, k, vt)


def _column_chunk_epilogue(step, first_step, n_chunks, tn, out_ref, x_ref, gate_row, y):
    for n in range(n_chunks):
        @pl.when(step == first_step + n)
        def _(n=n):
            cols = slice(n * tn, (n + 1) * tn)
            out_ref[:, cols] = x_ref[:, cols] + gate_row[:, cols] * y


def _merge_kernel(za_ref, o_ref, uc_ref, g0_ref, g1_ref, g2_ref, x_ref, mod_ref,
                  woa_ref, wo_ref, woc_ref, wm_ref, out_ref, m_ref, *, n_mix, tn):
    s = pl.program_id(1)
    tk = m_ref.shape[2]

    @pl.when(s < n_mix)
    def _():
        def sig(r):
            return jax.nn.sigmoid(r[...].astype(F32))

        ya = jnp.dot(za_ref[...], woa_ref[...], preferred_element_type=F32)
        yb = jnp.dot(o_ref[...], wo_ref[...], preferred_element_type=F32)
        yc = jnp.dot(uc_ref[...], woc_ref[...], preferred_element_type=F32)
        merged = sig(g0_ref) * ya + sig(g1_ref) * yb + sig(g2_ref) * yc
        m_ref[s] = merged.astype(BF16)

    @pl.when(s == n_mix)
    def _():
        gate1 = mod_ref[2:3, :]
        for n in range(D_MODEL // tn):
            cols = slice(n * tn, (n + 1) * tn)
            y = None
            for k in range(n_mix):
                part = jnp.dot(m_ref[k], wm_ref[k * tk:(k + 1) * tk, cols], preferred_element_type=F32)
                y = part if y is None else y + part
            out_ref[:, cols] = x_ref[:, cols] + gate1[:, cols] * y


def _merge_call(za, o, uc, p, x, mod_l, w_out_a, w_o, w_out_c, w_merge, row_fn, tm):
    t = x.shape[0]
    tn = 512
    n_mix = D_MODEL // tn
    gate_col0 = P_OFF_GATE // tn

    def mix(s):
        return jnp.minimum(s, n_mix - 1)

    def gate_spec(g):
        return pl.BlockSpec((tm, tn), lambda i, s: (i, gate_col0 + g * n_mix + mix(s)))

    return pl.pallas_call(
        functools.partial(_merge_kernel, n_mix=n_mix, tn=tn),
        grid=(t // tm, n_mix + 1),
        in_specs=[pl.BlockSpec((tm, D_CONV_A), lambda i, s: (i, 0)),
                  pl.BlockSpec((tm, D_MODEL), lambda i, s: (i, 0)),
                  pl.BlockSpec((tm, D_CONF), lambda i, s: (i, 0)),
                  gate_spec(0), gate_spec(1), gate_spec(2),
                  pl.BlockSpec((tm, D_MODEL), lambda i, s: (i, 0)),
                  pl.BlockSpec((None, 6, D_MODEL), lambda i, s: (row_fn(i * tm), 0, 0)),
                  pl.BlockSpec((D_CONV_A, tn), lambda i, s: (0, mix(s))),
                  pl.BlockSpec((D_MODEL, tn), lambda i, s: (0, mix(s))),
                  pl.BlockSpec((D_CONF, tn), lambda i, s: (0, mix(s))),
                  pl.BlockSpec((D_MODEL, D_MODEL), lambda i, s: (0, 0))],
        out_specs=pl.BlockSpec((tm, D_MODEL), lambda i, s: (i, 0)),
        out_shape=jax.ShapeDtypeStruct((t, D_MODEL), F32),
        scratch_shapes=[pltpu.VMEM((n_mix, tm, tn), BF16)],
        compiler_params=_cparams(2, 56),
        name="merge",
    )(za, o, uc, p, p, p, x, mod_l, w_out_a, w_o, w_out_c, w_merge)


def _ffn_kernel(*refs, final_norm, n_up, tn):
    if final_norm:
        x_ref, mod_ref, g_ref, w1_ref, w2_ref, fg_ref, out_ref, h_ref, a_ref = refs
    else:
        x_ref, mod_ref, g_ref, w1_ref, w2_ref, out_ref, h_ref, a_ref = refs
    s = pl.program_id(1)
    tf = a_ref.shape[2]
    n_down = D_MODEL // tn
    n_row_chunks = x_ref.shape[0] // ROW_CHUNK

    @pl.when(s == 0)
    def _():
        g = g_ref[...]
        shift = mod_ref[3:4, :]
        scale = mod_ref[4:5, :]

        def body(r, carry):
            rows = pl.ds(pl.multiple_of(r * ROW_CHUNK, ROW_CHUNK), ROW_CHUNK)
            h_ref[rows, :] = _modulated_rmsnorm_rows(x_ref[rows, :], g, scale, shift).astype(BF16)
            return carry

        lax.fori_loop(0, n_row_chunks, body, 0, unroll=ROW_UNROLL)

    @pl.when(s < n_up)
    def _():
        a = jnp.dot(h_ref[...], w1_ref[...], preferred_element_type=F32)
        a_ref[s] = jnp.square(jnp.maximum(a, 0.0)).astype(BF16)

    @pl.when(s >= n_up)
    def _():
        y = None
        for k in range(n_up):
            part = jnp.dot(a_ref[k], w2_ref[k * tf:(k + 1) * tf, :], preferred_element_type=F32)
            y = part if y is None else y + part
        _column_chunk_epilogue(s, n_up, n_down, tn, out_ref, x_ref, mod_ref[5:6, :], y)

    if final_norm:
        @pl.when(s == n_up + n_down - 1)
        def _():
            fg = fg_ref[...]

            def body(r, carry):
                rows = pl.ds(pl.multiple_of(r * ROW_CHUNK, ROW_CHUNK), ROW_CHUNK)
                y = out_ref[rows, :]
                ms = jnp.mean(y * y, axis=-1, keepdims=True)
                out_ref[rows, :] = y * lax.rsqrt(ms + NORM_EPS) * fg
                return carry

            lax.fori_loop(0, n_row_chunks, body, 0, unroll=ROW_UNROLL)


def _ffn_call(x, mod_l, g2, w1, w2, final_g, row_fn, tm):
    t = x.shape[0]
    tf = 1024
    tn = 512
    n_up = D_FF // tf
    final_norm = final_g is not None

    def up(s):
        return jnp.minimum(s, n_up - 1)

    def down(s):
        return jnp.maximum(s - n_up, 0)

    in_specs = [pl.BlockSpec((tm, D_MODEL), lambda i, s: (i, 0)),
                pl.BlockSpec((None, 6, D_MODEL), lambda i, s: (row_fn(i * tm), 0, 0)),
                pl.BlockSpec((1, D_MODEL), lambda i, s: (0, 0)),
                pl.BlockSpec((D_MODEL, tf), lambda i, s: (0, up(s))),
                pl.BlockSpec((D_FF, tn), lambda i, s: (0, down(s)))]
    args = [x, mod_l, g2, w1, w2]
    if final_norm:
        in_specs.append(pl.BlockSpec((1, D_MODEL), lambda i, s: (0, 0)))
        args.append(final_g)
    return pl.pallas_call(
        functools.partial(_ffn_kernel, final_norm=final_norm, n_up=n_up, tn=tn),
        grid=(t // tm, n_up + D_MODEL // tn),
        in_specs=in_specs,
        out_specs=pl.BlockSpec((tm, D_MODEL), lambda i, s: (i, 0)),
        out_shape=jax.ShapeDtypeStruct((t, D_MODEL), F32),
        scratch_shapes=[pltpu.VMEM((tm, D_MODEL), BF16), pltpu.VMEM((n_up, tm, tf), BF16)],
        compiler_params=_cparams(2, 60),
        name="ffn",
    )(*args)


def _cast_kernel(w_ref, o_ref):
    o_ref[...] = w_ref[...].astype(BF16)


def _cast_layer_call(w_stacked, l):
    _, rows, cols = w_stacked.shape
    tr = max(BF16_ROWS, min(rows, (4 * MIB) // (4 * cols)))
    assert rows % tr == 0
    return pl.pallas_call(
        _cast_kernel,
        grid=(rows // tr,),
        in_specs=[pl.BlockSpec((None, tr, cols), lambda i: (l, i, 0))],
        out_specs=pl.BlockSpec((tr, cols), lambda i: (i, 0)),
        out_shape=jax.ShapeDtypeStruct((rows, cols), BF16),
        compiler_params=_cparams(1, 32),
        name="cast",
    )(w_stacked)


def _repack_w_in_t(w_in, l):
    wt = jnp.swapaxes(w_in[l], 0, 1)
    w_q = wt[OFF_Q:OFF_KV].reshape(N_HEADS, QK_HEAD_DIM, D_MODEL)
    w_qn = w_q[:, :QK_NOPE_DIM].reshape(N_HEADS * QK_NOPE_DIM, D_MODEL)
    w_qp = w_q[:, QK_NOPE_DIM:].reshape(N_HEADS * QK_ROPE_DIM, D_MODEL)
    main_t = jnp.concatenate([wt[:OFF_Q], w_qn, wt[OFF_CONF:]], axis=0).astype(BF16)
    small_t = jnp.concatenate(
        [w_qp, wt[OFF_KV:OFF_CONF], jnp.zeros((LANES - QK_ROPE_DIM, D_MODEL), F32)], axis=0).astype(BF16)
    return main_t, small_t


def _rope_tables(n_tokens):
    pos = jnp.arange(n_tokens)
    row = (pos // GRID_W).astype(F32)
    col = (pos % GRID_W).astype(F32)
    inv_freq = ROPE_BASE ** (-jnp.arange(N_FREQ, dtype=F32) / N_FREQ)
    ang = jnp.stack([row[:, None] * inv_freq, col[:, None] * inv_freq], axis=1)
    cos, sin = jnp.cos(ang), jnp.sin(ang)
    zero = jnp.zeros_like(sin)
    cos64 = jnp.stack([cos, cos], axis=2).reshape(n_tokens, QK_ROPE_DIM)
    up64 = jnp.stack([-sin, zero], axis=2).reshape(n_tokens, QK_ROPE_DIM)
    dn64 = jnp.stack([zero, sin], axis=2).reshape(n_tokens, QK_ROPE_DIM)
    return tuple(jnp.concatenate([a, a], axis=1) for a in (cos64, up64, dn64))


def _prep_layer_weights(l, w_in, w_kv_b, w_o_attn, conf_dw_w):
    w_main, w_small = _repack_w_in_t(w_in, l)
    w_k =w_kv_b[l][:, :, :QK_NOPE_DIM].reshape(KV_LORA_RANK, N_HEADS * QK_NOPE_DIM).astype(BF16)
    w_vt = jnp.transpose(w_kv_b[l][:, :, QK_NOPE_DIM:], (1, 2, 0)).reshape(
        N_HEADS * V_HEAD_DIM, KV_LORA_RANK).astype(BF16)
    w_o = _cast_layer_call(w_o_attn.reshape(DEPTH, N_HEADS * V_HEAD_DIM, D_MODEL), l)
    wc = jnp.pad(conf_dw_w[l], ((0, 1), (0, 0)))
    wc = wc.reshape(CONF_WIDTH + 1, D_CONF // LANES, LANES).transpose(1, 0, 2)
    return w_main, w_small, w_k, w_vt, w_o, wc


def _run_group(x, seq_len, mods, row_fn, rope_tabs, caches, lw, final_g, tm, ts, tq):
    t = x.shape[0]
    b = t // seq_len
    ckv_layers, kpe_layers = [], []
    for l in range(DEPTH):
        w = lw[l]
        mod_l = mods[l]
        p, h = _inproj_call(x, mod_l, w["g1"], w["w_main"], row_fn, tm)
        outs = _qkvs_call(h, w["w_small"], w["kv_g"], rope_tabs, 512)
        qpe, ckv, kpe = outs[:3]
        ckv_layers.append(ckv)
        kpe_layers.append(kpe)
        za, uc = _mixers_call(p, w["conv_a_w"], w["wc"], w["conf_b"], w["ln_g"], w["ln_b"], seq_len, ts)
        ckv_all = ckv.reshape(b, seq_len, KV_LORA_RANK)
        if rope_tabs is None:
            kpe_all = kpe.reshape(b, seq_len, QK_ROPE_DIM)
        else:
            cache_ckv, cache_kpe = caches
            ckv_all = jnp.concatenate([ckv_all, cache_ckv[:, l]], axis=1)
            kpe_all = jnp.concatenate([outs[3].reshape(b, seq_len, QK_ROPE_DIM), cache_kpe[:, l]], axis=1)
        sk = ckv_all.shape[1]
        k, vt = _kvup_call(ckv_all, kpe_all, w["w_k"], w["w_vt"], 512 if sk % 512 == 0 else sk)
        o = _attn_call(p, qpe, k, vt, seq_len, tq)
        x = _merge_call(za, o, uc, p, x, mod_l, w["w_out_a"], w["w_o"], w["w_out_c"], w["w_merge"], row_fn, 512)
        x = _ffn_call(x, mod_l, w["g2"], w["w_ff1"], w["w_ff2"],
                      final_g if l == DEPTH - 1 else None, row_fn, 512)
    return x, ckv_layers, kpe_layers


def kernel(x_prompt, x_sample, cache_ckv, cache_kpe, c, c_ctx, w_mod, b_mod, norm1_g, w_in, conv_a_w, w_out_a, kv_norm_g, w_kv_b, w_o_attn, conf_dw_w, conf_dw_b, conf_ln_g, conf_ln_b, w_out_c, w_merge, norm2_g, w_ff1, w_ff2, final_norm_g):
    batch, seq, _ = x_prompt.shape
    dec_batch, dec_seq, _ = x_sample.shape

    n_rows = 8
    c_rows = jnp.concatenate(
        [c_ctx[None, :], c, jnp.zeros((n_rows - 1 - dec_batch, D_MODEL), F32)], axis=0)
    mod = _mod_call(c_rows, w_mod, b_mod).reshape(DEPTH, n_rows, 6, D_MODEL)
    mods = [mod[l] for l in range(DEPTH)]

    lw = []
    for l in range(DEPTH):
        w_main, w_small, w_k, w_vt, w_o, wc = _prep_layer_weights(l, w_in, w_kv_b, w_o_attn, conf_dw_w)
        lw.append(dict(
            g1=norm1_g[l][None, :], g2=norm2_g[l][None, :], kv_g=kv_norm_g[l][None, :],
            w_main=w_main, w_small=w_small, w_k=w_k, w_vt=w_vt, w_o=w_o, wc=wc,
            conv_a_w=conv_a_w[l], conf_b=conf_dw_b[l][None, :],
            ln_g=conf_ln_g[l][None, :], ln_b=conf_ln_b[l][None, :],
            w_out_a=_cast_layer_call(w_out_a, l), w_out_c=_cast_layer_call(w_out_c, l),
            w_merge=_cast_layer_call(w_merge, l),
            w_ff1=_cast_layer_call(w_ff1, l), w_ff2=_cast_layer_call(w_ff2, l)))
    final_g = final_norm_g[None, :]

    tm = 1024
    yp, ckv_layers, kpe_layers = _run_group(
        x_prompt.reshape(batch * seq, D_MODEL), seq, mods, lambda tok: 0, None, None,
        lw, final_g, tm, seq, seq)
    ys, _, _ = _run_group(
        x_sample.reshape(dec_batch * dec_seq, D_MODEL), dec_seq, mods,
        lambda tok: 1 + tok // dec_seq, _rope_tables(dec_seq), (cache_ckv, cache_kpe),
        lw, final_g, tm, 256, 256)

    new_ckv = jnp.stack([a.reshape(batch, seq, KV_LORA_RANK) for a in ckv_layers], axis=1)
    new_kpe = jnp.stack([a.reshape(batch, seq, QK_ROPE_DIM) for a in kpe_layers], axis=1)
    return (yp.reshape(batch, seq, D_MODEL), ys.reshape(dec_batch, dec_seq, D_MODEL), new_ckv, new_kpe)
```

```python
import functools

import jax
import jax.numpy as jnp
from jax import lax
from jax.experimental import pallas as pl
from jax.experimental.pallas import tpu as pltpu

F32 = jnp.float32
BF16 = jnp.bfloat16

D_MODEL = 2048
DEPTH = 2
GRID_W = 64
D_CONV_A = 1024
CONV_A_WIDTH = 3
N_HEADS = 16
QK_NOPE_DIM = 128
QK_ROPE_DIM = 64
V_HEAD_DIM = 128
KV_LORA_RANK = 512
QK_HEAD_DIM = QK_NOPE_DIM + QK_ROPE_DIM
N_FREQ = QK_ROPE_DIM // 4
ROPE_BASE = 10000.0
ATTN_SCALE = QK_HEAD_DIM ** -0.5
SCORE_SCALE = ATTN_SCALE * 1.4426950408889634
D_CONF = 1024
CONF_WIDTH = 31
D_FF = 4 * D_MODEL
NORM_EPS = 1e-6

SPLIT_A = 3 * D_CONV_A
SPLIT_Q = N_HEADS * QK_HEAD_DIM
SPLIT_KV = KV_LORA_RANK + QK_ROPE_DIM
SPLIT_CONF = 2 * D_CONF
OFF_Q = SPLIT_A
OFF_KV = OFF_Q + SPLIT_Q
OFF_CONF = OFF_KV + SPLIT_KV
OFF_GATE = OFF_CONF + SPLIT_CONF

LANES = 128
BF16_ROWS = 16
MIB = 1024 * 1024

P_OFF_A = 0
P_OFF_QN = SPLIT_A
P_OFF_CONF = P_OFF_QN + N_HEADS * QK_NOPE_DIM
P_OFF_GATE = P_OFF_CONF + SPLIT_CONF
P_WIDTH = P_OFF_GATE + 3 * D_MODEL
S_OFF_CKV = N_HEADS * QK_ROPE_DIM
S_OFF_KPE = S_OFF_CKV + KV_LORA_RANK
S_WIDTH = S_OFF_KPE + LANES
KEY_WIDTH = 2 * LANES

ROW_CHUNK = 16
ROW_UNROLL = 4
CONV_HALO = 16


def _cparams(n_axes, vmem_mib):
    return pltpu.CompilerParams(dimension_semantics=("arbitrary",) * n_axes,
                                vmem_limit_bytes=vmem_mib * MIB)


def _mod_kernel(c_ref, w_ref, b_ref, o_ref):
    c = c_ref[...]
    s = c * jax.nn.sigmoid(c)
    o_ref[...] = jnp.dot(s.astype(BF16), w_ref[...].astype(BF16),
                         preferred_element_type=F32) + b_ref[...]


def _mod_call(c_rows, w_mod, b_mod):
    n_rows = c_rows.shape[0]
    n = 6 * D_MODEL
    tn = 1024
    return pl.pallas_call(
        _mod_kernel,
        grid=(DEPTH, n // tn),
        in_specs=[pl.BlockSpec((n_rows, D_MODEL), lambda l, j: (0, 0)),
                  pl.BlockSpec((None, D_MODEL, tn), lambda l, j: (l, 0, j)),
                  pl.BlockSpec((None, 1, tn), lambda l, j: (l, 0, j))],
        out_specs=pl.BlockSpec((None, n_rows, tn), lambda l, j: (l, 0, j)),
        out_shape=jax.ShapeDtypeStruct((DEPTH, n_rows, n), F32),
        compiler_params=_cparams(2, 40),
        name="mod",
    )(c_rows, w_mod, b_mod.reshape(DEPTH, 1, n))


def _modulated_rmsnorm_rows(x, g, scale, shift):
    ms = jnp.mean(x * x, axis=-1, keepdims=True)
    y = x * lax.rsqrt(ms + NORM_EPS) * g
    return y * (1.0 + scale) + shift


def _dot_nt(a, b_t):
    return lax.dot_general(a, b_t, (((1,), (1,)), ((), ())), preferred_element_type=F32)


def _rope_lanes(x, cos, s_up, s_dn):
    up = pltpu.roll(x, LANES - N_FREQ, 1)
    dn = pltpu.roll(x, N_FREQ, 1)
    return x * cos + up * s_up + dn * s_dn


def _inproj_kernel(x_ref, mod_ref, g_ref, w_ref, p_ref, h_ref):
    @pl.when(pl.program_id(1) == 0)
    def _():
        g = g_ref[...]
        shift = mod_ref[0:1, :]
        scale = mod_ref[1:2, :]

        def body(r, carry):
            rows = pl.ds(pl.multiple_of(r * ROW_CHUNK, ROW_CHUNK), ROW_CHUNK)
            h = _modulated_rmsnorm_rows(x_ref[rows, :], g, scale, shift)
            h_ref[rows, :] = h.astype(BF16)
            return carry

        lax.fori_loop(0, x_ref.shape[0] // ROW_CHUNK, body, 0, unroll=ROW_UNROLL)

    p_ref[...] = _dot_nt(h_ref[...], w_ref[...]).astype(BF16)


def _inproj_call(x, mod_l, g1, w_main_t, row_fn, tm):
    t = x.shape[0]
    tn = 1024
    return pl.pallas_call(
        _inproj_kernel,
        grid=(t // tm, P_WIDTH // tn),
        in_specs=[pl.BlockSpec((tm, D_MODEL), lambda i, j: (i, 0)),
                  pl.BlockSpec((None, 6, D_MODEL), lambda i, j: (row_fn(i * tm), 0, 0)),
                  pl.BlockSpec((1, D_MODEL), lambda i, j: (0, 0)),
                  pl.BlockSpec((tn, D_MODEL), lambda i, j: (j, 0))],
        out_specs=[pl.BlockSpec((tm, tn), lambda i, j: (i, j)),
                   pl.BlockSpec((tm, D_MODEL), lambda i, j: (i, 0))],
        out_shape=[jax.ShapeDtypeStruct((t, P_WIDTH), BF16),
                   jax.ShapeDtypeStruct((t, D_MODEL), BF16)],
        compiler_params=_cparams(2, 48),
        name="inproj",
    )(x, mod_l, g1, w_main_t)


def _qkvs_kernel(*refs, rope):
    if rope:
        h_ref, w_ref, kvg_ref, cos_ref, sup_ref, sdn_ref, qpe_ref, ckv_ref, kpe_ref, kpr_ref = refs
    else:
        h_ref, w_ref, kvg_ref, qpe_ref, ckv_ref, kpe_ref = refs
    acc = _dot_nt(h_ref[...], w_ref[...])
    if rope:
        cos, s_up, s_dn = cos_ref[...], sup_ref[...], sdn_ref[...]
    for c in range(S_OFF_CKV // LANES):
        q = acc[:, c * LANES:(c + 1) * LANES]
        if rope:
            q = _rope_lanes(q, cos, s_up, s_dn)
        qpe_ref[:, c * LANES:(c + 1) * LANES] = q.astype(BF16)
    ckv = acc[:, S_OFF_CKV:S_OFF_KPE]
    ms = jnp.mean(ckv * ckv, axis=-1, keepdims=True)
    ckv_ref[...] = ckv * lax.rsqrt(ms + NORM_EPS) * kvg_ref[...]
    kpe2 = acc[:, S_OFF_KPE:S_WIDTH]
    kpe_ref[...] = kpe2[:, :QK_ROPE_DIM]
    if rope:
        kpr_ref[...] = _rope_lanes(kpe2, cos, s_up, s_dn)[:, :QK_ROPE_DIM]


def _qkvs_call(h, w_small_t, kv_g, rope_tabs, tm):
    t = h.shape[0]
    rope = rope_tabs is not None
    in_specs = [pl.BlockSpec((tm, D_MODEL), lambda i: (i, 0)),
                pl.BlockSpec((S_WIDTH, D_MODEL), lambda i: (0, 0)),
                pl.BlockSpec((1, KV_LORA_RANK), lambda i: (0, 0))]
    args = [h, w_small_t, kv_g]
    out_specs = [pl.BlockSpec((tm, S_OFF_CKV), lambda i: (i, 0)),
                 pl.BlockSpec((tm, KV_LORA_RANK), lambda i: (i, 0)),
                 pl.BlockSpec((tm, QK_ROPE_DIM), lambda i: (i, 0))]
    out_shape = [jax.ShapeDtypeStruct((t, S_OFF_CKV), BF16),
                 jax.ShapeDtypeStruct((t, KV_LORA_RANK), F32),
                 jax.ShapeDtypeStruct((t, QK_ROPE_DIM), F32)]
    if rope:
        seq_tiles = rope_tabs[0].shape[0] // tm
        in_specs += [pl.BlockSpec((tm, LANES), lambda i: (i % seq_tiles, 0))] * 3
        args += list(rope_tabs)
        out_specs.append(pl.BlockSpec((tm, QK_ROPE_DIM), lambda i: (i, 0)))
        out_shape.append(jax.ShapeDtypeStruct((t, QK_ROPE_DIM), F32))
    return pl.pallas_call(
        functools.partial(_qkvs_kernel, rope=rope),
        grid=(t // tm,),
        in_specs=in_specs, out_specs=out_specs, out_shape=out_shape,
        compiler_params=_cparams(1, 40),
        name="qkvs",
    )(*args)


def _mixers_kernel(xin, gb, gc, ca, cb, xin_p, gc_p, ca_p, cb_p, xin_n, gc_n, ca_n, cb_n,
                   wa_ref, wc_ref, cbias_ref, lng_ref, lnb_ref, za_ref, uc_ref,
                   bufa, bufc, convc, *, ts, tiles_per_seq):
    i = pl.program_id(0)
    first = (i % tiles_per_seq) == 0
    last = (i % tiles_per_seq) == tiles_per_seq - 1
    hl = CONV_HALO

    def f32(r):
        return r[...].astype(F32)

    bufa[0:hl, :] = jnp.where(first, 0.0, f32(gc_p) * f32(xin_p))
    bufa[hl:hl + ts, :] = f32(gc) * f32(xin)
    bufa[hl + ts:, :] = jnp.where(last, 0.0, f32(gc_n) * f32(xin_n))
    pad_a = (CONV_A_WIDTH - 1) // 2
    y = jnp.zeros((ts, D_CONV_A), F32)
    for k in range(CONV_A_WIDTH):
        y = y + wa_ref[k:k + 1, :] * bufa[pl.ds(hl - pad_a + k, ts), :]
    za_ref[...] = (f32(gb) * y).astype(BF16)

    def glu(a, b):
        return f32(a) * jax.nn.sigmoid(f32(b))

    n_chunks = D_CONF // LANES
    up = jnp.where(first, 0.0, glu(ca_p, cb_p))
    um = glu(ca, cb)
    un = jnp.where(last, 0.0, glu(ca_n, cb_n))
    for c in range(n_chunks):
        lanes = slice(c * LANES, (c + 1) * LANES)
        bufc[c, 0:hl, :] = up[:, lanes]
        bufc[c, hl:hl + ts, :] = um[:, lanes]
        bufc[c, hl + ts:, :] = un[:, lanes]
    pad_c = (CONF_WIDTH - 1) // 2

    def chunk_body(c, carry):
        w = wc_ref[c]
        acc = jnp.zeros((ts, LANES), F32)
        for k in range(CONF_WIDTH):
            acc = acc + w[k:k + 1, :] * bufc[c, pl.ds(hl - pad_c + k, ts), :]
        convc[c] = acc
        return carry

    lax.fori_loop(0, n_chunks, chunk_body, 0)
    u = jnp.concatenate([convc[c] for c in range(n_chunks)], axis=1) + cbias_ref[...]
    mu = jnp.mean(u, axis=-1, keepdims=True)
    uc = u - mu
    var = jnp.mean(uc * uc, axis=-1, keepdims=True)
    v = uc * lax.rsqrt(var + NORM_EPS) * lng_ref[...] + lnb_ref[...]
    uc_ref[...] = (v * jax.nn.sigmoid(v)).astype(BF16)


def _mixers_call(p, conv_a_w, conf_w_chunks, conf_b, ln_g, ln_b, seq_len, ts):
    t = p.shape[0]
    tiles_per_seq = seq_len // ts
    halo_per_tile = ts // CONV_HALO
    n_halo_blocks = t // CONV_HALO
    col_a = P_OFF_A // D_CONV_A
    col_c = P_OFF_CONF // D_CONF
    cols = [col_a, col_a + 1, col_a + 2, col_c, col_c + 1]
    halo_cols = [col_a, col_a + 2, col_c, col_c + 1]

    def main_spec(col):
        return pl.BlockSpec((ts, D_CONF), lambda i: (i, col))

    def prev_spec(col):
        return pl.BlockSpec((CONV_HALO, D_CONF),
                            lambda i: (jnp.maximum(i * halo_per_tile - 1, 0), col))

    def next_spec(col):
        return pl.BlockSpec((CONV_HALO, D_CONF),
                            lambda i: (jnp.minimum((i + 1) * halo_per_tile, n_halo_blocks - 1), col))

    def full(a):
        return pl.BlockSpec(a.shape, lambda i: (0,) * a.ndim)

    small = [conv_a_w, conf_w_chunks, conf_b, ln_g, ln_b]
    in_specs = ([main_spec(c) for c in cols] + [prev_spec(c) for c in halo_cols]
                + [next_spec(c) for c in halo_cols] + [full(a) for a in small])
    return pl.pallas_call(
        functools.partial(_mixers_kernel, ts=ts, tiles_per_seq=tiles_per_seq),
        grid=(t // ts,),
        in_specs=in_specs,
        out_specs=[pl.BlockSpec((ts, D_CONV_A), lambda i: (i, 0)),
                   pl.BlockSpec((ts, D_CONF), lambda i: (i, 0))],
        out_shape=[jax.ShapeDtypeStruct((t, D_CONV_A), BF16),
                   jax.ShapeDtypeStruct((t, D_CONF), BF16)],
        scratch_shapes=[pltpu.VMEM((ts + 2 * CONV_HALO, D_CONV_A), F32),
                        pltpu.VMEM((D_CONF // LANES, ts + 2 * CONV_HALO, LANES), F32),
                        pltpu.VMEM((D_CONF // LANES, ts, LANES), F32)],
        compiler_params=_cparams(1, 40),
        name="mixers",
    )(*([p] * 13), *small)


def _kvup_kernel(*refs, with_cache):
    if with_cache:
        ckv_ref, kpe_ref, cckv_ref, ckpe_ref, wk_ref, wvt_ref, k_ref, vt_ref = refs
        from_cache = pl.program_id(1) == pl.num_programs(1) - 1
        ckv = jnp.where(from_cache, cckv_ref[...], ckv_ref[...]).astype(BF16)
        kpe = (jnp.where(from_cache, ckpe_ref[...], kpe_ref[...]) * SCORE_SCALE).astype(BF16)
    else:
        ckv_ref, kpe_ref, wk_ref, wvt_ref, k_ref, vt_ref = refs
        ckv = ckv_ref[...].astype(BF16)
        kpe = (kpe_ref[...] * SCORE_SCALE).astype(BF16)
    zeros = jnp.zeros_like(kpe)
    kn = (jnp.dot(ckv, wk_ref[...], preferred_element_type=F32) * SCORE_SCALE).astype(BF16)
    vt = lax.dot_general(wvt_ref[...], ckv, (((1,), (1,)), ((), ())),
                         preferred_element_type=F32).astype(BF16)
    for h in range(N_HEADS):
        k_ref[h, :, 0:QK_NOPE_DIM] = kn[:, h * QK_NOPE_DIM:(h + 1) * QK_NOPE_DIM]
        lo, hi = (kpe, zeros) if h % 2 == 0 else (zeros, kpe)
        k_ref[h, :, QK_NOPE_DIM:QK_NOPE_DIM + QK_ROPE_DIM] = lo
        k_ref[h, :, QK_NOPE_DIM + QK_ROPE_DIM:] = hi
        vt_ref[h] = vt[h * V_HEAD_DIM:(h + 1) * V_HEAD_DIM, :]


def _kvup_call(ckv_new, kpe_new, caches, l, w_k, w_vt, tk):
    b, seq_len, _ = ckv_new.shape
    n_new = seq_len // tk
    with_cache = caches is not None
    sk = seq_len + (tk if with_cache else 0)

    def new_tile(s):
        return jnp.minimum(s, n_new - 1)

    in_specs = [pl.BlockSpec((None, tk, KV_LORA_RANK), lambda bi, s: (bi, new_tile(s), 0)),
                pl.BlockSpec((None, tk, QK_ROPE_DIM), lambda bi, s: (bi, new_tile(s), 0))]
    args = [ckv_new, kpe_new]
    if with_cache:
        assert caches[0].shape[2] == tk and caches[1].shape[2] == tk
        in_specs += [pl.BlockSpec((None, None, tk, KV_LORA_RANK), lambda bi, s: (bi, l, 0, 0)),
                     pl.BlockSpec((None, None, tk, QK_ROPE_DIM), lambda bi, s: (bi, l, 0, 0))]
        args += list(caches)
    return pl.pallas_call(
        functools.partial(_kvup_kernel, with_cache=with_cache),
        grid=(b, sk // tk),
        in_specs=in_specs + [
                  pl.BlockSpec(w_k.shape, lambda bi, s: (0, 0)),
                  pl.BlockSpec(w_vt.shape, lambda bi, s: (0, 0))],
        out_specs=[pl.BlockSpec((None, N_HEADS, tk, KEY_WIDTH), lambda bi, s: (bi, 0, s, 0)),
                   pl.BlockSpec((None, N_HEADS, V_HEAD_DIM, tk), lambda bi, s: (bi, 0, 0, s))],
        out_shape=[jax.ShapeDtypeStruct((b, N_HEADS, sk, KEY_WIDTH), BF16),
                   jax.ShapeDtypeStruct((b, N_HEADS, V_HEAD_DIM, sk), BF16)],
        compiler_params=_cparams(2, 40),
        name="kvup",
    )(*args, w_k, w_vt)


def _attn_kernel(qn_ref, qp_ref, k_ref, vt_ref, o_ref):
    for h in range(k_ref.shape[0]):
        pair = (h // 2) * LANES
        q = jnp.concatenate([qn_ref[:, h * QK_NOPE_DIM:(h + 1) * QK_NOPE_DIM],
                             qp_ref[:, pair:pair + LANES]], axis=1)
        st = lax.dot_general(k_ref[h], q, (((1,), (1,)), ((), ())), preferred_element_type=F32)
        m = jnp.max(st, axis=0, keepdims=True)
        p = jnp.exp2(st - m)
        l = jnp.sum(p, axis=0, keepdims=True)
        ot = jnp.dot(vt_ref[h], p.astype(BF16), preferred_element_type=F32)
        o_ref[:, h * V_HEAD_DIM:(h + 1) * V_HEAD_DIM] = (ot * (1.0 / l)).T.astype(BF16)


def _attn_pipe_kernel(qn_ref, qp_ref, k_ref, vt_ref, o_ref, s_scr, m_scr, *, ck):
    sk = k_ref.shape[1]

    @pl.when(pl.program_id(2) == 0)
    def _():
        s_scr[...] = jnp.zeros_like(s_scr)
        m_scr[...] = jnp.zeros_like(m_scr)

    for h in range(k_ref.shape[0]):
        pair = (h // 2) * LANES
        q = jnp.concatenate([qn_ref[:, h * QK_NOPE_DIM:(h + 1) * QK_NOPE_DIM],
                             qp_ref[:, pair:pair + LANES]], axis=1)
        m_old = m_scr[h, 0:1, :]
        m_new = None
        l = None
        acc = None
        for c in range(sk // ck):
            rows = pl.ds(c * ck, ck)
            p = jnp.exp2(s_scr[h, rows, :] - m_old)
            lc = jnp.sum(p, axis=0, keepdims=True)
            pv = jnp.dot(vt_ref[h, :, rows], p.astype(BF16), preferred_element_type=F32)
            st = lax.dot_general(k_ref[h, rows, :], q, (((1,), (1,)), ((), ())),
                                 preferred_element_type=F32)
            s_scr[h, rows, :] = st
            mc = jnp.max(st, axis=0, keepdims=True)
            l = lc if l is None else l + lc
            acc = pv if acc is None else acc + pv
            m_new = mc if m_new is None else jnp.maximum(m_new, mc)
        m_scr[h] = jnp.broadcast_to(m_new, m_scr.shape[1:])
        o_ref[:, h * V_HEAD_DIM:(h + 1) * V_HEAD_DIM] = (acc * (1.0 / l)).T.astype(BF16)


def _attn_pipe_call(p, qpe, k, vt, seq_len, tq):
    t = p.shape[0]
    sk = k.shape[2]
    q_tiles = seq_len // tq
    hps = 4
    assert P_OFF_QN % (hps * QK_NOPE_DIM) == 0
    qn_col0 = P_OFF_QN // (hps * QK_NOPE_DIM)

    def q_row(b, qi):
        return b * q_tiles + jnp.minimum(qi, q_tiles - 1)

    def o_row(b, qi):
        return b * q_tiles + jnp.maximum(qi - 1, 0)

    return pl.pallas_call(
        functools.partial(_attn_pipe_kernel, ck=512),
        grid=(t // seq_len, N_HEADS // hps, q_tiles + 1),
        in_specs=[pl.BlockSpec((tq, hps * QK_NOPE_DIM), lambda b, hp, qi: (q_row(b, qi), qn_col0 + hp)),
                  pl.BlockSpec((tq, hps * QK_ROPE_DIM), lambda b, hp, qi: (q_row(b, qi), hp)),
                  pl.BlockSpec((None, hps, sk, KEY_WIDTH), lambda b, hp, qi: (b, hp, 0, 0)),
                  pl.BlockSpec((None, hps, V_HEAD_DIM, sk), lambda b, hp, qi: (b, hp, 0, 0))],
        out_specs=pl.BlockSpec((tq, hps * V_HEAD_DIM), lambda b, hp, qi: (o_row(b, qi), hp)),
        out_shape=jax.ShapeDtypeStruct((t, N_HEADS * V_HEAD_DIM), BF16),
        scratch_shapes=[pltpu.VMEM((hps, sk, tq), F32), pltpu.VMEM((hps, 8, tq), F32)],
        compiler_params=_cparams(3, 60),
        name="attn_pipe",
    )(p, qpe, k, vt)


def _attn_call(p, qpe, k, vt, seq_len, tq):
    t = p.shape[0]
    sk = k.shape[2]
    q_tiles = seq_len // tq
    if q_tiles > 1:
        return _attn_pipe_call(p, qpe, k, vt, seq_len, tq)
    hps = 8
    assert P_OFF_QN % (hps * QK_NOPE_DIM) == 0
    qn_col0 = P_OFF_QN // (hps * QK_NOPE_DIM)
    return pl.pallas_call(
        _attn_kernel,
        grid=(t // seq_len, N_HEADS // hps, q_tiles),
        in_specs=[pl.BlockSpec((tq, hps * QK_NOPE_DIM), lambda b, hp, qi: (b * q_tiles + qi, qn_col0 + hp)),
                  pl.BlockSpec((tq, hps * QK_ROPE_DIM), lambda b, hp, qi: (b * q_tiles + qi, hp)),
                  pl.BlockSpec((None, hps, sk, KEY_WIDTH), lambda b, hp, qi: (b, hp, 0, 0)),
                  pl.BlockSpec((None, hps, V_HEAD_DIM, sk), lambda b, hp, qi: (b, hp, 0, 0))],
        out_specs=pl.BlockSpec((tq, hps * V_HEAD_DIM), lambda b, hp, qi: (b * q_tiles + qi, hp)),
        out_shape=jax.ShapeDtypeStruct((t, N_HEADS * V_HEAD_DIM), BF16),
        compiler_params=_cparams(3, 48),
        name="attn",
    )(p, qpe, k, vt)


def _column_chunk_epilogue(step, first_step, n_chunks, tn, out_ref, x_ref, gate_row, y):
    for n in range(n_chunks):
        @pl.when(step == first_step + n)
        def _(n=n):
            cols = slice(n * tn, (n + 1) * tn)
            out_ref[:, cols] = x_ref[:, cols] + gate_row[:, cols] * y


def _merge_kernel(za_ref, o_ref, uc_ref, g0_ref, g1_ref, g2_ref, x_ref, mod_ref,
                  woa_ref, wo_ref, woc_ref, wm_ref, out_ref, m_ref, *, n_mix, tn):
    s = pl.program_id(1)
    tk = m_ref.shape[2]

    @pl.when(s < n_mix)
    def _():
        def sig(r):
            return jax.nn.sigmoid(r[...].astype(F32))

        ya = jnp.dot(za_ref[...], woa_ref[...], preferred_element_type=F32)
        yb = jnp.dot(o_ref[...], wo_ref[...], preferred_element_type=F32)
        yc = jnp.dot(uc_ref[...], woc_ref[...], preferred_element_type=F32)
        merged = sig(g0_ref) * ya + sig(g1_ref) * yb + sig(g2_ref) * yc
        m_ref[s] = merged.astype(BF16)

    @pl.when(s == n_mix)
    def _():
        gate1 = mod_ref[2:3, :]
        for n in range(D_MODEL // tn):
            cols = slice(n * tn, (n + 1) * tn)
            y = None
            for k in range(n_mix):
                part = jnp.dot(m_ref[k], wm_ref[k * tk:(k + 1) * tk, cols], preferred_element_type=F32)
                y = part if y is None else y + part
            out_ref[:, cols] = x_ref[:, cols] + gate1[:, cols] * y


def _merge_call(za, o, uc, p, x, mod_l, w_out_a, w_o, w_out_c, w_merge, row_fn, tm):
    t = x.shape[0]
    tn = 512
    n_mix = D_MODEL // tn
    gate_col0 = P_OFF_GATE // tn

    def mix(s):
        return jnp.minimum(s, n_mix - 1)

    def gate_spec(g):
        return pl.BlockSpec((tm, tn), lambda i, s: (i, gate_col0 + g * n_mix + mix(s)))

    return pl.pallas_call(
        functools.partial(_merge_kernel, n_mix=n_mix, tn=tn),
        grid=(t // tm, n_mix + 1),
        in_specs=[pl.BlockSpec((tm, D_CONV_A), lambda i, s: (i, 0)),
                  pl.BlockSpec((tm, D_MODEL), lambda i, s: (i, 0)),
                  pl.BlockSpec((tm, D_CONF), lambda i, s: (i, 0)),
                  gate_spec(0), gate_spec(1), gate_spec(2),
                  pl.BlockSpec((tm, D_MODEL), lambda i, s: (i, 0)),
                  pl.BlockSpec((None, 6, D_MODEL), lambda i, s: (row_fn(i * tm), 0, 0)),
                  pl.BlockSpec((D_CONV_A, tn), lambda i, s: (0, mix(s))),
                  pl.BlockSpec((D_MODEL, tn), lambda i, s: (0, mix(s))),
                  pl.BlockSpec((D_CONF, tn), lambda i, s: (0, mix(s))),
                  pl.BlockSpec((D_MODEL, D_MODEL), lambda i, s: (0, 0))],
        out_specs=pl.BlockSpec((tm, D_MODEL), lambda i, s: (i, 0)),
        out_shape=jax.ShapeDtypeStruct((t, D_MODEL), F32),
        scratch_shapes=[pltpu.VMEM((n_mix, tm, tn), BF16)],
        compiler_params=_cparams(2, 56),
        name="merge",
    )(za, o, uc, p, p, p, x, mod_l, w_out_a, w_o, w_out_c, w_merge)


def _ffn_kernel(*refs, final_norm, n_up, tn):
    if final_norm:
        x_ref, mod_ref, g_ref, w1_ref, w2_ref, fg_ref, out_ref, h_ref, a_ref = refs
    else:
        x_ref, mod_ref, g_ref, w1_ref, w2_ref, out_ref, h_ref, a_ref = refs
    s = pl.program_id(1)
    tf = a_ref.shape[2]
    n_down = D_MODEL // tn
    n_row_chunks = x_ref.shape[0] // ROW_CHUNK

    @pl.when(s == 0)
    def _():
        g = g_ref[...]
        shift = mod_ref[3:4, :]
        scale = mod_ref[4:5, :]

        def body(r, carry):
            rows = pl.ds(pl.multiple_of(r * ROW_CHUNK, ROW_CHUNK), ROW_CHUNK)
            h_ref[rows, :] = _modulated_rmsnorm_rows(x_ref[rows, :], g, scale, shift).astype(BF16)
            return carry

        lax.fori_loop(0, n_row_chunks, body, 0, unroll=ROW_UNROLL)

    @pl.when(s < n_up)
    def _():
        a = jnp.dot(h_ref[...], w1_ref[...], preferred_element_type=F32)
        a_ref[s] = jnp.square(jnp.maximum(a, 0.0)).astype(BF16)

    @pl.when(s >= n_up)
    def _():
        y = None
        for k in range(n_up):
            part = jnp.dot(a_ref[k], w2_ref[k * tf:(k + 1) * tf, :], preferred_element_type=F32)
            y = part if y is None else y + part
        _column_chunk_epilogue(s, n_up, n_down, tn, out_ref, x_ref, mod_ref[5:6, :], y)

    if final_norm:
        @pl.when(s == n_up + n_down - 1)
        def _():
            fg = fg_ref[...]

            def body(r, carry):
                rows = pl.ds(pl.multiple_of(r * ROW_CHUNK, ROW_CHUNK), ROW_CHUNK)
                y = out_ref[rows, :]
                ms = jnp.mean(y * y, axis=-1, keepdims=True)
                out_ref[rows, :] = y * lax.rsqrt(ms + NORM_EPS) * fg
                return carry

            lax.fori_loop(0, n_row_chunks, body, 0, unroll=ROW_UNROLL)


def _ffn_call(x, mod_l, g2, w1, w2, final_g, row_fn, tm):
    t = x.shape[0]
    tf = 1024
    tn = 512
    n_up = D_FF // tf
    final_norm = final_g is not None

    def up(s):
        return jnp.minimum(s, n_up - 1)

    def down(s):
        return jnp.maximum(s - n_up, 0)

    in_specs = [pl.BlockSpec((tm, D_MODEL), lambda i, s: (i, 0)),
                pl.BlockSpec((None, 6, D_MODEL), lambda i, s: (row_fn(i * tm), 0, 0)),
                pl.BlockSpec((1, D_MODEL), lambda i, s: (0, 0)),
                pl.BlockSpec((D_MODEL, tf), lambda i, s: (0, up(s))),
                pl.BlockSpec((D_FF, tn), lambda i, s: (0, down(s)))]
    args = [x, mod_l, g2, w1, w2]
    if final_norm:
        in_specs.append(pl.BlockSpec((1, D_MODEL), lambda i, s: (0, 0)))
        args.append(final_g)
    return pl.pallas_call(
        functools.partial(_ffn_kernel, final_norm=final_norm, n_up=n_up, tn=tn),
        grid=(t // tm, n_up + D_MODEL // tn),
        in_specs=in_specs,
        out_specs=pl.BlockSpec((tm, D_MODEL), lambda i, s: (i, 0)),
        out_shape=jax.ShapeDtypeStruct((t, D_MODEL), F32),
        scratch_shapes=[pltpu.VMEM((tm, D_MODEL), BF16), pltpu.VMEM((n_up, tm, tf), BF16)],
        compiler_params=_cparams(2, 60),
        name="ffn",
    )(*args)


def _cast_kernel(w_ref, o_ref):
    o_ref[...] = w_ref[...].astype(BF16)


def _cast_layer_call(w_stacked, l):
    _, rows, cols = w_stacked.shape
    tr = max(BF16_ROWS, min(rows, (4 * MIB) // (4 * cols)))
    assert rows % tr == 0
    return pl.pallas_call(
        _cast_kernel,
        grid=(rows // tr,),
        in_specs=[pl.BlockSpec((None, tr, cols), lambda i: (l, i, 0))],
        out_specs=pl.BlockSpec((tr, cols), lambda i: (i, 0)),
        out_shape=jax.ShapeDtypeStruct((rows, cols), BF16),
        compiler_params=_cparams(1, 32),
        name="cast",
    )(w_stacked)


def _repack_w_in_t(w_in, l):
    wt = jnp.swapaxes(w_in[l], 0, 1)
    w_q = wt[OFF_Q:OFF_KV].reshape(N_HEADS, QK_HEAD_DIM, D_MODEL)
    w_qn = w_q[:, :QK_NOPE_DIM].reshape(N_HEADS * QK_NOPE_DIM, D_MODEL)
    w_qp = w_q[:, QK_NOPE_DIM:].reshape(N_HEADS * QK_ROPE_DIM, D_MODEL)
    main_t = jnp.concatenate([wt[:OFF_Q], w_qn, wt[OFF_CONF:]], axis=0).astype(BF16)
    small_t = jnp.concatenate(
        [w_qp, wt[OFF_KV:OFF_CONF], jnp.zeros((LANES - QK_ROPE_DIM, D_MODEL), F32)], axis=0).astype(BF16)
    return main_t, small_t


def _rope_tables(n_tokens):
    pos = jnp.arange(n_tokens)
    row = (pos // GRID_W).astype(F32)
    col = (pos % GRID_W).astype(F32)
    inv_freq = ROPE_BASE ** (-jnp.arange(N_FREQ, dtype=F32) / N_FREQ)
    ang = jnp.stack([row[:, None] * inv_freq, col[:, None] * inv_freq], axis=1)
    cos, sin = jnp.cos(ang), jnp.sin(ang)
    zero = jnp.zeros_like(sin)
    cos64 = jnp.stack([cos, cos], axis=2).reshape(n_tokens, QK_ROPE_DIM)
    up64 = jnp.stack([-sin, zero], axis=2).reshape(n_tokens, QK_ROPE_DIM)
    dn64 = jnp.stack([zero, sin], axis=2).reshape(n_tokens, QK_ROPE_DIM)
    return tuple(jnp.concatenate([a, a], axis=1) for a in (cos64, up64, dn64))


def _prep_layer_weights(l, w_in, w_kv_b, w_o_attn, conf_dw_w):
    w_main, w_small = _repack_w_in_t(w_in, l)
    w_k =w_kv_b[l][:, :, :QK_NOPE_DIM].reshape(KV_LORA_RANK, N_HEADS * QK_NOPE_DIM).astype(BF16)
    w_vt = jnp.transpose(w_kv_b[l][:, :, QK_NOPE_DIM:], (1, 2, 0)).reshape(
        N_HEADS * V_HEAD_DIM, KV_LORA_RANK).astype(BF16)
    w_o = _cast_layer_call(w_o_attn.reshape(DEPTH, N_HEADS * V_HEAD_DIM, D_MODEL), l)
    wc = jnp.pad(conf_dw_w[l], ((0, 1), (0, 0)))
    wc = wc.reshape(CONF_WIDTH + 1, D_CONF // LANES, LANES).transpose(1, 0, 2)
    return w_main, w_small, w_k, w_vt, w_o, wc


def _tiles(seq_len):
    return dict(
        inproj=1024,
        qkvs=512, merge=512, ffn=512,
        conv=min(seq_len, 256),
        attn_q=min(seq_len, 256),
        kvup=min(seq_len, 512))


def _run_group(x, seq_len, mods, row_fn, rope_tabs, caches, lw, final_g):
    t = x.shape[0]
    b = t // seq_len
    tiles = _tiles(seq_len)
    ckv_layers, kpe_layers = [], []
    for l in range(DEPTH):
        w = lw[l]
        mod_l = mods[l]
        p, h = _inproj_call(x, mod_l, w["g1"], w["w_main"], row_fn, tiles["inproj"])
        outs = _qkvs_call(h, w["w_small"], w["kv_g"], rope_tabs, tiles["qkvs"])
        qpe, ckv, kpe = outs[:3]
        ckv_layers.append(ckv)
        kpe_layers.append(kpe)
        za, uc = _mixers_call(p, w["conv_a_w"], w["wc"], w["conf_b"], w["ln_g"], w["ln_b"],
                              seq_len, tiles["conv"])
        kpe_keys = kpe if rope_tabs is None else outs[3]
        k, vt = _kvup_call(ckv.reshape(b, seq_len, KV_LORA_RANK), kpe_keys.reshape(b, seq_len, QK_ROPE_DIM),
                           caches, l, w["w_k"], w["w_vt"], tiles["kvup"])
        o = _attn_call(p, qpe, k, vt, seq_len, tiles["attn_q"])
        x = _merge_call(za, o, uc, p, x, mod_l, w["w_out_a"], w["w_o"], w["w_out_c"], w["w_merge"],
                        row_fn, tiles["merge"])
        x = _ffn_call(x, mod_l, w["g2"], w["w_ff1"], w["w_ff2"],
                      final_g if l == DEPTH - 1 else None, row_fn, tiles["ffn"])
    return x, ckv_layers, kpe_layers


def kernel(x_prompt, x_sample, cache_ckv, cache_kpe, c, c_ctx, w_mod, b_mod, norm1_g, w_in, conv_a_w, w_out_a, kv_norm_g, w_kv_b, w_o_attn, conf_dw_w, conf_dw_b, conf_ln_g, conf_ln_b, w_out_c, w_merge, norm2_g, w_ff1, w_ff2, final_norm_g):
    batch, seq, _ = x_prompt.shape
    dec_batch, dec_seq, _ = x_sample.shape

    n_rows = 8
    c_rows = jnp.concatenate(
        [c_ctx[None, :], c, jnp.zeros((n_rows - 1 - dec_batch, D_MODEL), F32)], axis=0)
    mod = _mod_call(c_rows, w_mod, b_mod).reshape(DEPTH, n_rows, 6, D_MODEL)
    mods = [mod[l] for l in range(DEPTH)]

    lw = []
    for l in range(DEPTH):
        w_main, w_small, w_k, w_vt, w_o, wc = _prep_layer_weights(l, w_in, w_kv_b, w_o_attn, conf_dw_w)
        lw.append(dict(
            g1=norm1_g[l][None, :], g2=norm2_g[l][None, :], kv_g=kv_norm_g[l][None, :],
            w_main=w_main, w_small=w_small, w_k=w_k, w_vt=w_vt, w_o=w_o, wc=wc,
            conv_a_w=conv_a_w[l], conf_b=conf_dw_b[l][None, :],
            ln_g=conf_ln_g[l][None, :], ln_b=conf_ln_b[l][None, :],
            w_out_a=_cast_layer_call(w_out_a, l), w_out_c=_cast_layer_call(w_out_c, l),
            w_merge=_cast_layer_call(w_merge, l),
            w_ff1=_cast_layer_call(w_ff1, l), w_ff2=_cast_layer_call(w_ff2, l)))
    final_g = final_norm_g[None, :]

    yp, ckv_layers, kpe_layers = _run_group(
        x_prompt.reshape(batch * seq, D_MODEL), seq, mods, lambda tok: 0, None, None, lw, final_g)
    ys, _, _ = _run_group(
        x_sample.reshape(dec_batch * dec_seq, D_MODEL), dec_seq, mods,
        lambda tok: 1 + tok // dec_seq, _rope_tables(dec_seq), (cache_ckv, cache_kpe), lw, final_g)

    new_ckv = jnp.stack([a.reshape(batch, seq, KV_LORA_RANK) for a in ckv_layers], axis=1)
    new_kpe = jnp.stack([a.reshape(batch, seq, QK_ROPE_DIM) for a in kpe_layers], axis=1)
    return (yp.reshape(batch, seq, D_MODEL), ys.reshape(dec_batch, dec_seq, D_MODEL), new_ckv, new_kpe)
```
